```python
import math, functools
import jax, jax.numpy as jnp
from jax import lax
import numpy as np

D_MODEL = 1024
BATCH = 2
SEQ = 16384
DEPTH = 1
DEC_BATCH = 128
DEC_SEQ = 1
PAST_LEN = 8192
PAGE_SIZE = 128

MIX_WIDTH = D_MODEL
ATTN_WIDTH = MIX_WIDTH // 2
SSM_WIDTH = MIX_WIDTH - ATTN_WIDTH
HEAD_DIM = 64
N_ATTN_HEADS = ATTN_WIDTH // HEAD_DIM
SSM_GROUP = 16
N_SSM_GROUPS = SSM_WIDTH // SSM_GROUP
SSM_STATE = 64
IN_WIDTH = 3 * ATTN_WIDTH + SSM_WIDTH
D_FF = 4 * D_MODEL
Q_BLOCK = 128
RMS_EPS = 1e-6
DT_MIN = 1e-3
DT_MAX = 1e-1
SB_BIAS_INIT = -6.0

kernel_name = 'hymba_stickbreak_s5_sqrelu_step'


def rmsnorm(x, g):
    xf = x.astype(jnp.float32)
    inv = lax.rsqrt(jnp.mean(xf * xf, axis=-1, keepdims=True) + RMS_EPS)
    return (xf * inv).astype(x.dtype) * g


def ada_modulation(c, w_ada, b_ada):
    mod = jax.nn.silu(c) @ w_ada + b_ada
    return jnp.split(mod[:, None, :], 6, axis=-1)


def sb_key_block(q, k, v, bias, log_carry, acc, mask):
    z = jnp.einsum('bthd,bshd->bhts', q, k) * (HEAD_DIM ** -0.5) + bias[None, :, None, None]
    log_beta = jax.nn.log_sigmoid(z)
    log_1m = jnp.where(mask, jax.nn.log_sigmoid(-z), 0.0)
    incl = jnp.cumsum(log_1m, axis=-1)
    blk_tot = incl[..., -1]
    log_a = log_beta + (blk_tot[..., None] - incl) + log_carry[..., None]
    a = jnp.where(mask, jnp.exp(log_a), 0.0)
    acc = acc + jnp.einsum('bhts,bshd->bthd', a, v)
    return log_carry + blk_tot, acc


def sb_attention_prompt(q, k, v, bias):
    b, s, h, dh = q.shape
    n_blk = s // Q_BLOCK
    offs = jnp.arange(Q_BLOCK)

    def one_query_block(i):
        q_blk = lax.dynamic_slice_in_dim(q, i * Q_BLOCK, Q_BLOCK, axis=1)
        qpos = i * Q_BLOCK + offs

        def body(n, carry):
            j = i - n
            k_blk = lax.dynamic_slice_in_dim(k, j * Q_BLOCK, Q_BLOCK, axis=1)
            v_blk = lax.dynamic_slice_in_dim(v, j * Q_BLOCK, Q_BLOCK, axis=1)
            mask = (j * Q_BLOCK + offs)[None, :] < qpos[:, None]
            return sb_key_block(q_blk, k_blk, v_blk, bias, carry[0], carry[1], mask)

        init = (jnp.zeros((b, h, Q_BLOCK), jnp.float32),
                jnp.zeros((b, Q_BLOCK, h, dh), jnp.float32))
        _, acc = lax.fori_loop(0, i + 1, body, init)
        return acc

    out = lax.map(one_query_block, jnp.arange(n_blk))
    return out.transpose(1, 0, 2, 3, 4).reshape(b, s, h, dh)


def sb_attention_sample(q, k_new, v_new, bias, k_pool, v_pool, page_table):
    db, t, h, dh = q.shape
    tpos = jnp.arange(t)
    init_c = jnp.zeros((db, h, t), jnp.float32)
    init_acc = jnp.zeros((db, t, h, dh), jnp.float32)
    log_c, acc = sb_key_block(q, k_new, v_new, bias, init_c, init_acc, tpos[None, :] < tpos[:, None])
    full = jnp.ones((t, k_pool.shape[1]), bool)

    def page_step(carry, phys):
        k_pg = k_pool[phys].astype(jnp.float32)
        v_pg = v_pool[phys].astype(jnp.float32)
        return sb_key_block(q, k_pg, v_pg, bias, carry[0], carry[1], full), None

    (log_c, acc), _ = lax.scan(page_step, (log_c, acc), page_table.T, reverse=True)
    return acc


def _linear_combine(left, right):
    a_l, b_l = left
    a_r, b_r = right
    return a_r * a_l, a_r * b_l + b_r


def s5_mixer(u, h0_re, h0_im, lam_re, lam_im, log_dt, b_re, b_im, c_re, c_im, d_skip, w_glu, b_glu):
    f32 = jnp.float32
    bsz, t, _ = u.shape
    uf = u.astype(f32).reshape(bsz, t, N_SSM_GROUPS, SSM_GROUP)
    lam = lax.complex(lam_re.astype(f32), lam_im.astype(f32))
    dt = jnp.exp(log_dt.astype(f32))[:, None]
    lam_bar = jnp.exp(lam * dt)
    b_bar = ((lam_bar - 1.0) / lam)[..., None] * lax.complex(b_re.astype(f32), b_im.astype(f32))
    bu = jnp.einsum('btgc,gpc->btgp', uf.astype(jnp.complex64), b_bar)
    h0 = lax.complex(h0_re.astype(f32), h0_im.astype(f32))
    bu = bu.at[:, 0].add(lam_bar * h0)
    a = jnp.broadcast_to(lam_bar, bu.shape)
    _, hs = lax.associative_scan(_linear_combine, (a, bu), axis=1)
    y = (jnp.einsum('btgp,gcp->btgc', hs.real, c_re.astype(f32))
         - jnp.einsum('btgp,gcp->btgc', hs.imag, c_im.astype(f32))
         + d_skip.astype(f32) * uf)
    y = jax.nn.gelu(y.reshape(bsz, t, SSM_WIDTH))
    y = y * jax.nn.sigmoid(y @ w_glu.astype(f32) + b_glu.astype(f32))
    return y.astype(u.dtype), hs[:, -1].real, hs[:, -1].imag


def decoder_layer(x, c, attn_fn, h0_re, h0_im, p):
    bsz, t, _ = x.shape
    f32 = jnp.float32
    sh1, sc1, g1, sh2, sc2, g2 = ada_modulation(c, p['w_ada'], p['b_ada'])
    h = rmsnorm(x, p['g_pre_mix']) * (1 + sc1) + sh1
    proj = h @ p['w_in']
    q, k, v, u = jnp.split(proj, [ATTN_WIDTH, 2 * ATTN_WIDTH, 3 * ATTN_WIDTH], axis=-1)
    q = q.reshape(bsz, t, N_ATTN_HEADS, HEAD_DIM)
    k = k.reshape(bsz, t, N_ATTN_HEADS, HEAD_DIM)
    v = v.reshape(bsz, t, N_ATTN_HEADS, HEAD_DIM)
    o_attn = attn_fn(q.astype(f32), k.astype(f32), v.astype(f32),
                     p['sb_bias'].astype(f32)).reshape(bsz, t, ATTN_WIDTH).astype(x.dtype)
    o_ssm, h_re, h_im = s5_mixer(u, h0_re, h0_im, p['ssm_lam_re'], p['ssm_lam_im'], p['ssm_log_dt'],
                                 p['ssm_b_re'], p['ssm_b_im'], p['ssm_c_re'], p['ssm_c_im'],
                                 p['ssm_d'], p['w_glu'], p['b_glu'])
    merged = jnp.concatenate([rmsnorm(o_attn, p['g_attn_out']), rmsnorm(o_ssm, p['g_ssm_out'])], axis=-1)
    x = x + g1 * rmsnorm(merged @ p['w_out'], p['g_post_mix'])
    h = rmsnorm(x, p['g_pre_ffn']) * (1 + sc2) + sh2
    f = jnp.square(jax.nn.relu(h @ p['w_up'])) @ p['w_down']
    x = x + g2 * rmsnorm(f, p['g_post_ffn'])
    return x, k, v, h_re, h_im


def setup_inputs(seed: int = 0) -> dict:
    key = jax.random.key(seed)
    ks = jax.random.split(key, 40)
    f32 = jnp.float32
    n_pages = PAST_LEN // PAGE_SIZE
    used = DEC_BATCH * n_pages
    n_pool = used + max(1, used // 4)

    def nrm(k, shape, scale):
        return jax.random.normal(k, shape, f32) * scale

    def gain(k, n):
        return 1.0 + 0.02 * jax.random.normal(k, (DEPTH, n), f32)

    page_table = jax.random.permutation(ks[9], n_pool)[:used].reshape(DEC_BATCH, n_pages).astype(jnp.int32)
    lam_im = (jnp.pi * jnp.broadcast_to(jnp.arange(SSM_STATE, dtype=f32), (DEPTH, N_SSM_GROUPS, SSM_STATE))
              + nrm(ks[13], (DEPTH, N_SSM_GROUPS, SSM_STATE), 0.01))
    return {
        'x_prompt': nrm(ks[0], (BATCH, SEQ, D_MODEL), 1.0),
        'x_sample': nrm(ks[1], (DEC_BATCH, DEC_SEQ, D_MODEL), 1.0),
        'c_prompt': nrm(ks[2], (BATCH, D_MODEL), 1.0),
        'c_sample': nrm(ks[3], (DEC_BATCH, D_MODEL), 1.0),
        'cache_k': nrm(ks[4], (DEPTH, n_pool, PAGE_SIZE, N_ATTN_HEADS, HEAD_DIM), 1.0),
        'cache_v': nrm(ks[5], (DEPTH, n_pool, PAGE_SIZE, N_ATTN_HEADS, HEAD_DIM), 1.0),
        'state_ssm_re': nrm(ks[6], (DEPTH, DEC_BATCH, N_SSM_GROUPS, SSM_STATE), 0.5),
        'state_ssm_im': nrm(ks[7], (DEPTH, DEC_BATCH, N_SSM_GROUPS, SSM_STATE), 0.5),
        'page_table': page_table,
        'w_ada': nrm(ks[10], (DEPTH, D_MODEL, 6 * D_MODEL), 0.5 * D_MODEL ** -0.5),
        'b_ada': nrm(ks[11], (DEPTH, 6 * D_MODEL), 0.01),
        'g_pre_mix': gain(ks[12], D_MODEL),
        'w_in': nrm(ks[14], (DEPTH, D_MODEL, IN_WIDTH), D_MODEL ** -0.5),
        'sb_bias': SB_BIAS_INIT + nrm(ks[32], (DEPTH, N_ATTN_HEADS), 0.1),
        'ssm_lam_re': -0.5 + nrm(ks[15], (DEPTH, N_SSM_GROUPS, SSM_STATE), 0.01),
        'ssm_lam_im': lam_im,
        'ssm_log_dt': jax.random.uniform(ks[16], (DEPTH, N_SSM_GROUPS), f32,
                                         minval=math.log(DT_MIN), maxval=math.log(DT_MAX)),
        'ssm_b_re': nrm(ks[17], (DEPTH, N_SSM_GROUPS, SSM_STATE, SSM_GROUP), (2 * SSM_GROUP) ** -0.5),
        'ssm_b_im': nrm(ks[18], (DEPTH, N_SSM_GROUPS, SSM_STATE, SSM_GROUP), (2 * SSM_GROUP) ** -0.5),
        'ssm_c_re': nrm(ks[19], (DEPTH, N_SSM_GROUPS, SSM_GROUP, SSM_STATE), (2 * SSM_STATE) ** -0.5),
        'ssm_c_im': nrm(ks[20], (DEPTH, N_SSM_GROUPS, SSM_GROUP, SSM_STATE), (2 * SSM_STATE) ** -0.5),
        'ssm_d': nrm(ks[21], (DEPTH, N_SSM_GROUPS, SSM_GROUP), 0.5),
        'w_glu': nrm(ks[22], (DEPTH, SSM_WIDTH, SSM_WIDTH), SSM_WIDTH ** -0.5),
        'b_glu': nrm(ks[23], (DEPTH, SSM_WIDTH), 0.01),
        'g_attn_out': gain(ks[24], ATTN_WIDTH),
        'g_ssm_out': gain(ks[25], SSM_WIDTH),
        'w_out': nrm(ks[26], (DEPTH, MIX_WIDTH, D_MODEL), MIX_WIDTH ** -0.5),
        'g_post_mix': gain(ks[27], D_MODEL),
        'g_pre_ffn': gain(ks[28], D_MODEL),
        'w_up': nrm(ks[29], (DEPTH, D_MODEL, D_FF), D_MODEL ** -0.5),
        'w_down': nrm(ks[30], (DEPTH, D_FF, D_MODEL), D_FF ** -0.5),
        'g_post_ffn': gain(ks[31], D_MODEL),
    }


def reference(x_prompt, x_sample, c_prompt, c_sample, cache_k, cache_v, state_ssm_re, state_ssm_im,
              page_table, w_ada, b_ada, g_pre_mix, w_in, sb_bias, ssm_lam_re, ssm_lam_im, ssm_log_dt,
              ssm_b_re, ssm_b_im, ssm_c_re, ssm_c_im, ssm_d, w_glu, b_glu, g_attn_out, g_ssm_out,
              w_out, g_post_mix, g_pre_ffn, w_up, w_down, g_post_ffn):
    xp, xs = x_prompt, x_sample
    zero_state = jnp.zeros((x_prompt.shape[0], N_SSM_GROUPS, SSM_STATE), jnp.float32)
    kp_l, vp_l, srp_l, sip_l = [], [], [], []
    ks_l, vs_l, srs_l, sis_l = [], [], [], []
    for l in range(DEPTH):
        p = dict(w_ada=w_ada[l], b_ada=b_ada[l], g_pre_mix=g_pre_mix[l], w_in=w_in[l], sb_bias=sb_bias[l],
                 ssm_lam_re=ssm_lam_re[l], ssm_lam_im=ssm_lam_im[l], ssm_log_dt=ssm_log_dt[l],
                 ssm_b_re=ssm_b_re[l], ssm_b_im=ssm_b_im[l], ssm_c_re=ssm_c_re[l], ssm_c_im=ssm_c_im[l],
                 ssm_d=ssm_d[l], w_glu=w_glu[l], b_glu=b_glu[l], g_attn_out=g_attn_out[l],
                 g_ssm_out=g_ssm_out[l], w_out=w_out[l], g_post_mix=g_post_mix[l],
                 g_pre_ffn=g_pre_ffn[l], w_up=w_up[l], w_down=w_down[l], g_post_ffn=g_post_ffn[l])
        xp, kp, vp, hrp, hip = decoder_layer(xp, c_prompt, sb_attention_prompt, zero_state, zero_state, p)
        sample_attn = functools.partial(sb_attention_sample, k_pool=cache_k[l], v_pool=cache_v[l],
                                        page_table=page_table)
        xs, kn, vn, hrs, his = decoder_layer(xs, c_sample, sample_attn, state_ssm_re[l], state_ssm_im[l], p)
        kp_l.append(kp); vp_l.append(vp); srp_l.append(hrp); sip_l.append(hip)
        ks_l.append(kn); vs_l.append(vn); srs_l.append(hrs); sis_l.append(his)
    k_prompt = jnp.stack(kp_l)
    v_prompt = jnp.stack(vp_l)
    ssm_re_prompt = jnp.stack(srp_l)
    ssm_im_prompt = jnp.stack(sip_l)
    k_sample = jnp.stack(ks_l)
    v_sample = jnp.stack(vs_l)
    ssm_re_sample = jnp.stack(srs_l)
    ssm_im_sample = jnp.stack(sis_l)
    return (xp, xs, k_prompt, v_prompt, ssm_re_prompt, ssm_im_prompt,
            k_sample, v_sample, ssm_re_sample, ssm_im_sample)
```

```python
import functools
import math

import numpy as np
import jax
import jax.numpy as jnp
from jax import lax
from jax.experimental import pallas as pl
from jax.experimental.pallas import tpu as pltpu

F32 = jnp.float32
BF16 = jnp.bfloat16
HIGHEST = lax.Precision.HIGHEST

RMS_EPS = 1e-6
HEAD_DIM = 64
SSM_GROUP = 16
SSM_STATE = 64
LOG2E = 1.4426950408889634
LN2 = 0.6931471805599453

LANES = 128
SUBLANES = 8
KEY_BLOCK = 128
Q_WIDTH = 512
PROJ_ROWS = 512
SSM_ROWS = 256
PAGES_PER_STEP = 8
SSM_UNIT = 16
VMEM_LIMIT = 56 * 1024 * 1024


def _cparams(n_axes):
    return pltpu.CompilerParams(dimension_semantics=("arbitrary",) * n_axes,
                                vmem_limit_bytes=VMEM_LIMIT)


def _rms(x, g):
    inv = lax.rsqrt(jnp.mean(x * x, axis=-1, keepdims=True) + RMS_EPS)
    return (x * inv) * g


def _dot(a, b, precise):
    if precise:
        return jnp.dot(a, b, precision=HIGHEST, preferred_element_type=F32)
    return jnp.dot(a.astype(BF16), b.astype(BF16), preferred_element_type=F32)


def _softplus2(z2):
    neg_abs = lax.bitcast_convert_type(
        lax.bitcast_convert_type(z2, jnp.uint32) | jnp.uint32(0x80000000), F32)
    e = jnp.exp2(neg_abs)
    return jnp.maximum(z2, 0.0) + jnp.log(1.0 + e) * LOG2E


def _split_bf16(p):
    hi = lax.bitcast_convert_type(
        lax.bitcast_convert_type(p, jnp.uint32) & jnp.uint32(0xFFFF0000), F32)
    return hi.astype(BF16), (p - hi).astype(BF16)


def _prep_kernel(lre_ref, lim_ref, dt_ref, bre_ref, bim_ref, lre8_ref, lim8_ref, dt8_ref,
                 bbre_ref, bbim_ref, tab_ref):
    def lam_bar(lre, lim, log_dt):
        dt = jnp.exp(log_dt)
        mag = jnp.exp(lre * dt)
        ang = lim * dt
        return mag * jnp.cos(ang), mag * jnp.sin(ang)

    lre, lim = lre_ref[...], lim_ref[...]
    lbr, lbi = lam_bar(lre, lim, dt_ref[...])
    nr, ni = lbr - 1.0, lbi
    den = lre * lre + lim * lim
    cre = (nr * lre + ni * lim) / den
    cim = (ni * lre - nr * lim) / den
    bre, bim = bre_ref[...], bim_ref[...]
    bbre_ref[...] = cre * bre - cim * bim
    bbim_ref[...] = cre * bim + cim * bre

    p1r, p1i = lam_bar(lre8_ref[...], lim8_ref[...], dt8_ref[...])
    row = lax.broadcasted_iota(jnp.int32, p1r.shape, 0)

    def cmul(ar, ai, br, bi):
        return ar * br - ai * bi, ar * bi + ai * br

    p2r, p2i = cmul(p1r, p1i, p1r, p1i)
    p4r, p4i = cmul(p2r, p2i, p2r, p2i)
    p8r, p8i = cmul(p4r, p4i, p4r, p4i)
    e = row + 1
    accr, acci = jnp.ones_like(p1r), jnp.zeros_like(p1r)
    for bit, (pr, pi) in enumerate(((p1r, p1i), (p2r, p2i), (p4r, p4i), (p8r, p8i))):
        on = ((e >> bit) & 1) == 1
        nr_, ni_ = cmul(accr, acci, pr, pi)
        accr = jnp.where(on, nr_, accr)
        acci = jnp.where(on, ni_, acci)
    zero = jnp.zeros_like(p1r)
    tab_ref[0] = jnp.where(row >= 1, p1r, zero)
    tab_ref[1] = jnp.where(row >= 1, p1i, zero)
    tab_ref[2] = jnp.where(row >= 2, p2r, zero)
    tab_ref[3] = jnp.where(row >= 2, p2i, zero)
    tab_ref[4] = jnp.where(row >= 4, p4r, zero)
    tab_ref[5] = jnp.where(row >= 4, p4i, zero)
    tab_ref[6] = accr
    tab_ref[7] = acci


def _prep(lam_re, lam_im, log_dt, b_re, b_im):
    g, p = lam_re.shape
    c = b_re.shape[-1]
    rows = g * c

    def rep(a):
        return jnp.broadcast_to(a[:, None, :], (g, c, p)).reshape(rows, p)

    def flat8(a):
        return jnp.broadcast_to(a.reshape(1, g * p), (SUBLANES, g * p))

    dt_gp = jnp.broadcast_to(log_dt[:, None], (g, p))
    bt = lambda b: b.transpose(0, 2, 1).reshape(rows, p)
    bbre, bbim, tab = pl.pallas_call(
        _prep_kernel,
        out_shape=(jax.ShapeDtypeStruct((rows, p), F32),
                   jax.ShapeDtypeStruct((rows, p), F32),
                   jax.ShapeDtypeStruct((8, SUBLANES, g * p), F32)),
        name="prep",
    )(rep(lam_re), rep(lam_im), rep(dt_gp), bt(b_re), bt(b_im),
      flat8(lam_re), flat8(lam_im), flat8(dt_gp))
    return bbre.reshape(g, c, p), bbim.reshape(g, c, p), tab


def _ada_kernel(c_ref, w_ref, b_ref, o_ref):
    c = c_ref[...]
    s = c * jax.nn.sigmoid(c)
    o_ref[...] = _dot(s, w_ref[...], True) + b_ref[...]


def _ada(c_all, w_ada, b_ada):
    rows, d = c_all.shape
    n = w_ada.shape[1]
    bn = 1024
    return pl.pallas_call(
        _ada_kernel,
        grid=(n // bn,),
        in_specs=[pl.BlockSpec((rows, d), lambda j: (0, 0)),
                  pl.BlockSpec((d, bn), lambda j: (0, j)),
                  pl.BlockSpec((1, bn), lambda j: (0, j))],
        out_specs=pl.BlockSpec((rows, bn), lambda j: (0, j)),
        out_shape=jax.ShapeDtypeStruct((rows, n), F32),
        compiler_params=_cparams(1),
        name="ada",
    )(c_all, w_ada, b_ada.reshape(1, n))


def _proj_kernel(x_ref, sh_ref, sc_ref, g_ref, w_ref, *out_refs, attn_w, precise, q_scale):
    h = _rms(x_ref[...], g_ref[...]) * (1.0 + sc_ref[...]) + sh_ref[...]
    p = _dot(h, w_ref[...], precise)
    q, k, v, u = (p[:, i * attn_w:(i + 1) * attn_w] for i in range(4))
    if precise:
        q_ref, k_ref, v_ref, u_ref = out_refs
        q_ref[...] = q * q_scale
    else:
        q_ref, k_ref, v_ref, u_ref, kb_ref, vb_ref = out_refs
        q_ref[...] = (q * q_scale).astype(BF16)
        kb_ref[...] = k.astype(BF16)
        vb_ref[...] = v.astype(BF16)
    k_ref[...] = k
    v_ref[...] = v
    u_ref[...] = u


def _proj_prompt(x, mod, g_pre, w_in_bf, q_scale):
    b, t, d = x.shape
    n = w_in_bf.shape[1]
    aw = n // 4
    tb = min(PROJ_ROWS, t)
    row = lambda bi, i: (bi, i, 0)
    modspec = lambda which: pl.BlockSpec((None, None, 1, d), lambda bi, i: (bi, which, 0, 0))
    f32o = jax.ShapeDtypeStruct((b, t, aw), F32)
    bfo = jax.ShapeDtypeStruct((b, t, aw), BF16)
    ospec = pl.BlockSpec((None, tb, aw), row)
    return pl.pallas_call(
        functools.partial(_proj_kernel, attn_w=aw, precise=False, q_scale=q_scale),
        grid=(b, t // tb),
        in_specs=[pl.BlockSpec((None, tb, d), row), modspec(0), modspec(1),
                  pl.BlockSpec((1, d), lambda bi, i: (0, 0)),
                  pl.BlockSpec((d, n), lambda bi, i: (0, 0))],
        out_specs=(ospec,) * 6,
        out_shape=(bfo, f32o, f32o, f32o, bfo, bfo),
        compiler_params=_cparams(2),
        name="proj_prompt",
    )(x, mod, mod, g_pre, w_in_bf)


def _proj_sample(x, sh, sc, g_pre, w_in, q_scale):
    rows, d = x.shape
    n = w_in.shape[1]
    aw = n // 4
    full = lambda shape: pl.BlockSpec(shape, lambda i: (0,) * len(shape))
    o = jax.ShapeDtypeStruct((rows, aw), F32)
    return pl.pallas_call(
        functools.partial(_proj_kernel, attn_w=aw, precise=True, q_scale=q_scale),
        grid=(1,),
        in_specs=[full((rows, d)), full((rows, d)), full((rows, d)), full((1, d)), full((d, n))],
        out_specs=(full((rows, aw)),) * 4,
        out_shape=(o, o, o, o),
        compiler_params=_cparams(1),
        name="proj_sample",
    )(x, sh, sc, g_pre, w_in)


def _neg_suffix_matrix(rows):
    s = np.arange(rows)[:, None]
    j = np.arange(2 * KEY_BLOCK)[None, :] % KEY_BLOCK
    return jnp.asarray(np.where((j >= s) | (s >= KEY_BLOCK), -1.0, 0.0), dtype=BF16)


NU_ROWS = KEY_BLOCK + 16


def _attn_p_kernel(bias_ref, qt_ref, k_ref, vt_ref, nu_ref, o_ref,
                   qm_s, z_s, phl_s, r_s, a_s, acc_s, carry_s):
    hp = pl.program_id(1)
    qi = pl.program_id(2)
    qw = qt_ref.shape[1]
    n_chunks = qw // LANES
    blocks_per_q = qw // KEY_BLOCK

    q32 = qt_ref[...].astype(F32)
    row = lax.broadcasted_iota(jnp.int32, q32.shape, 0)
    for hh in range(2):
        own = (row >= hh * HEAD_DIM) & (row < (hh + 1) * HEAD_DIM)
        qm_s[hh] = jnp.where(own, q32, 0.0).astype(BF16)
        acc_s[hh] = jnp.zeros(acc_s.shape[1:], F32)
        carry_s[hh] = jnp.zeros(carry_s.shape[1:], F32)

    def key_block(j, masked):
        k0 = pl.multiple_of(j * KEY_BLOCK, KEY_BLOCK)
        kj = k_ref[pl.ds(k0, KEY_BLOCK), :]
        if masked:
            thr = qi * qw - j * KEY_BLOCK
            s_io = lax.broadcasted_iota(jnp.int32, (KEY_BLOCK, LANES), 0)
            t_io = lax.broadcasted_iota(jnp.int32, (KEY_BLOCK, LANES), 1)
        for hh in range(2):
            bias2 = bias_ref[2 * hp + hh]
            z_s[hh] = jnp.dot(kj, qm_s[hh], preferred_element_type=F32) + bias2
            for c in range(n_chunks):
                sl = slice(c * LANES, (c + 1) * LANES)
                p = _softplus2(z_s[hh, :, sl])
                if masked:
                    vis = (s_io - t_io) < (thr + c * LANES)
                    p = jnp.where(vis, p, 0.0)
                hi, lo = _split_bf16(p)
                phl_s[hh, 0:KEY_BLOCK, sl] = hi
                phl_s[hh, KEY_BLOCK:2 * KEY_BLOCK, sl] = lo
            r_s[hh] = jnp.dot(nu_ref[...], phl_s[hh], preferred_element_type=F32)
            for c in range(n_chunks):
                sl = slice(c * LANES, (c + 1) * LANES)
                log_a = z_s[hh, :, sl] + r_s[hh, 0:KEY_BLOCK, sl] + carry_s[hh, 0:1, sl]
                a = jnp.exp2(log_a)
                if masked:
                    vis = (s_io - t_io) < (thr + c * LANES)
                    a = jnp.where(vis, a, 0.0)
                a_s[hh, :, sl] = a.astype(BF16)
            vt = vt_ref[j, hh * HEAD_DIM:(hh + 1) * HEAD_DIM, :]
            acc_s[hh] += jnp.dot(vt, a_s[hh], preferred_element_type=F32)
            carry_s[hh] += r_s[hh, KEY_BLOCK:KEY_BLOCK + SUBLANES, :]

    last = (qi + 1) * blocks_per_q - 1

    def diag_body(n, c):
        key_block(last - n, True)
        return c

    lax.fori_loop(0, blocks_per_q, diag_body, 0)

    def full_body(n, c):
        key_block(qi * blocks_per_q - 1 - n, False)
        return c

    lax.fori_loop(0, qi * blocks_per_q, full_body, 0)

    for hh in range(2):
        o_ref[hh * HEAD_DIM:(hh + 1) * HEAD_DIM, :] = acc_s[hh]


def _attn_prompt(bias2, qt, kb, vb):
    b, w, t = qt.shape
    qw = min(Q_WIDTH, t)
    pair = 2 * HEAD_DIM
    nkb = t // KEY_BLOCK
    vt = vb.reshape(b, nkb, KEY_BLOCK, w).transpose(0, 1, 3, 2)
    return pl.pallas_call(
        _attn_p_kernel,
        grid=(b, w // pair, t // qw),
        in_specs=[pl.BlockSpec(memory_space=pltpu.SMEM),
                  pl.BlockSpec((None, pair, qw), lambda bi, hp, qi: (bi, hp, qi)),
                  pl.BlockSpec((None, t, pair), lambda bi, hp, qi: (bi, 0, hp)),
                  pl.BlockSpec((None, nkb, pair, KEY_BLOCK), lambda bi, hp, qi: (bi, 0, hp, 0)),
                  pl.BlockSpec((NU_ROWS, 2 * KEY_BLOCK), lambda bi, hp, qi: (0, 0))],
        out_specs=pl.BlockSpec((None, pair, qw), lambda bi, hp, qi: (bi, hp, qi)),
        out_shape=jax.ShapeDtypeStruct((b, w, t), F32),
        scratch_shapes=[pltpu.VMEM((2, pair, qw), BF16),
                        pltpu.VMEM((2, KEY_BLOCK, qw), F32),
                        pltpu.VMEM((2, 2 * KEY_BLOCK, qw), BF16),
                        pltpu.VMEM((2, NU_ROWS, qw), F32),
                        pltpu.VMEM((2, KEY_BLOCK, qw), BF16),
                        pltpu.VMEM((2, HEAD_DIM, qw), F32),
                        pltpu.VMEM((2, SUBLANES, qw), F32)],
        compiler_params=_cparams(3),
        name="attn_prompt",
    )(bias2, qt, kb, vt, _neg_suffix_matrix(NU_ROWS))


def _neg_suffix_matrix_lanes():
    j = np.arange(2 * KEY_BLOCK)[:, None] % KEY_BLOCK
    s = np.arange(2 * KEY_BLOCK)[None, :]
    return jnp.asarray(np.where((j >= s) | (s >= KEY_BLOCK), -1.0, 0.0), dtype=BF16)


def _attn_s_kernel(pt_ref, q_ref, kn_ref, vn_ref, bias_ref, nu_ref, *refs, n_heads):
    pages = PAGES_PER_STEP
    k_refs, v_refs = refs[:pages], refs[pages:2 * pages]
    o_ref = refs[2 * pages]
    qx_s, acc_s, carry_s = refs[2 * pages + 1:]
    g = pl.program_id(1)
    w = q_ref.shape[-1]
    rows = qx_s.shape[0]
    lane_head = lax.broadcasted_iota(jnp.int32, (rows, w), 1) // HEAD_DIM
    row_io = lax.broadcasted_iota(jnp.int32, (rows, w), 0)
    own = lane_head == row_io

    @pl.when(g == 0)
    def _():
        qx = jnp.where(own, jnp.broadcast_to(q_ref[0], (rows, w)), 0.0)
        qx_s[...] = qx.astype(BF16)
        tpos = lax.broadcasted_iota(jnp.int32, (rows, 1), 1)
        vis = tpos < tpos
        z_new = jnp.sum(qx * kn_ref[0], axis=-1, keepdims=True) + bias_ref[...]
        sp = _softplus2(z_new)
        log_1m = jnp.where(vis, -sp, 0.0)
        a_new = jnp.where(vis, jnp.exp2(z_new - sp), 0.0)
        acc_s[...] = a_new * jnp.where(own, jnp.broadcast_to(vn_ref[0], (rows, w)), 0.0)
        carry_s[...] = jnp.broadcast_to(log_1m, carry_s.shape)

    qx = qx_s[...]
    for r in range(pages):
        kp = k_refs[r][0].astype(BF16)
        z = lax.dot_general(qx, kp, (((1,), (1,)), ((), ())),
                            preferred_element_type=F32) + bias_ref[...]
        hi, lo = _split_bf16(_softplus2(z))
        rr = jnp.dot(jnp.concatenate([hi, lo], axis=1), nu_ref[...],
                     preferred_element_type=F32)
        a = jnp.exp2(z + rr[:, :KEY_BLOCK] + carry_s[...])
        acc_s[...] += jnp.dot(a.astype(BF16), v_refs[r][0].astype(BF16),
                              preferred_element_type=F32)
        carry_s[...] += rr[:, KEY_BLOCK:]

    @pl.when(g == pl.num_programs(1) - 1)
    def _():
        o_ref[0] = jnp.sum(jnp.where(own, acc_s[...], 0.0), axis=0, keepdims=True)


def _attn_sample(page_table, q2, k_new, v_new, bias2, pool_k, pool_v, n_heads):
    db, w = q2.shape
    n_pages = page_table.shape[1]
    pages = PAGES_PER_STEP
    assert n_pages % pages == 0
    page_rows = pool_k.shape[1]
    assert page_rows == KEY_BLOCK
    rows = 16
    bias_col = jnp.zeros((rows, 1), F32).at[:n_heads, 0].set(bias2)
    tok = lambda a: a.reshape(db, 1, w)
    tokspec = pl.BlockSpec((1, 1, w), lambda s, g, pt: (s, 0, 0))

    def page_spec(r):
        return pl.BlockSpec((1, page_rows, w),
                            lambda s, g, pt: (pt[s, n_pages - 1 - (g * pages + r)], 0, 0))

    grid_spec = pltpu.PrefetchScalarGridSpec(
        num_scalar_prefetch=1,
        grid=(db, n_pages // pages),
        in_specs=[tokspec, tokspec, tokspec,
                  pl.BlockSpec((rows, 1), lambda s, g, pt: (0, 0)),
                  pl.BlockSpec((2 * KEY_BLOCK, 2 * KEY_BLOCK), lambda s, g, pt: (0, 0))]
                 + [page_spec(r) for r in range(pages)] * 2,
        out_specs=tokspec,
        scratch_shapes=[pltpu.VMEM((rows, w), BF16),
                        pltpu.VMEM((rows, w), F32),
                        pltpu.VMEM((rows, KEY_BLOCK), F32)])
    out = pl.pallas_call(
        functools.partial(_attn_s_kernel, n_heads=n_heads),
        grid_spec=grid_spec,
        out_shape=jax.ShapeDtypeStruct((db, 1, w), F32),
        compiler_params=_cparams(2),
        name="attn_sample",
    )(page_table, tok(q2), tok(k_new), tok(v_new), bias_col, _neg_suffix_matrix_lanes(),
      *([pool_k] * pages), *([pool_v] * pages))
    return out.reshape(db, w)


def _gelu_glu(y, wg_ref, bg_ref, precise):
    g = 0.5 * y * (1.0 + jnp.tanh(math.sqrt(2.0 / math.pi) * (y + 0.044715 * (y * y * y))))
    return g * jax.nn.sigmoid(_dot(g, wg_ref[...], precise) + bg_ref[...])


def _ssm_p_kernel(u_ref, w_ref, cm_ref, tab_ref, d_ref, wg_ref, bg_ref,
                  o_ref, hre_ref, him_ref, bu_s, h_s):
    step = pl.program_id(0)
    nb, tc, width = u_ref.shape
    n_units = w_ref.shape[0]
    uw = width // n_units
    sw = w_ref.shape[2] // 2
    n_tiles = tc // SUBLANES

    @pl.when(step == 0)
    def _():
        h_s[:, :, 0:SUBLANES, :] = jnp.zeros((nb, n_units, SUBLANES, 2 * sw), F32)

    @pl.when(step > 0)
    def _():
        h_s[:, :, 0:SUBLANES, :] = h_s[:, :, tc:tc + SUBLANES, :]

    for b in range(nb):
        for m in range(n_units):
            bu_s[b, m] = _dot(u_ref[b, :, m * uw:(m + 1) * uw], w_ref[m], False)

    def tile_body(tb, carry):
        r0 = pl.multiple_of(tb * SUBLANES, SUBLANES)
        for b in range(nb):
            for m in range(n_units):
                for c in range(sw // LANES):
                    re = slice(c * LANES, (c + 1) * LANES)
                    im = slice(sw + c * LANES, sw + (c + 1) * LANES)
                    tl = slice(m * sw + c * LANES, m * sw + (c + 1) * LANES)
                    xr = bu_s[b, m, pl.ds(r0, SUBLANES), re]
                    xi = bu_s[b, m, pl.ds(r0, SUBLANES), im]
                    for k, ti in ((1, 0), (2, 2), (4, 4)):
                        ar, ai = tab_ref[ti, :, tl], tab_ref[ti + 1, :, tl]
                        sr = pltpu.roll(xr, k, 0)
                        si = pltpu.roll(xi, k, 0)
                        xr, xi = xr + (ar * sr - ai * si), xi + (ar * si + ai * sr)
                    prev_r = h_s[b, m, pl.ds(r0, SUBLANES), re]
                    prev_i = h_s[b, m, pl.ds(r0, SUBLANES), im]
                    pr = jnp.broadcast_to(prev_r[SUBLANES - 1:SUBLANES, :], (SUBLANES, LANES))
                    pi = jnp.broadcast_to(prev_i[SUBLANES - 1:SUBLANES, :], (SUBLANES, LANES))
                    lr, li = tab_ref[6, :, tl], tab_ref[7, :, tl]
                    xr, xi = xr + (lr * pr - li * pi), xi + (lr * pi + li * pr)
                    h_s[b, m, pl.ds(r0 + SUBLANES, SUBLANES), re] = xr
                    h_s[b, m, pl.ds(r0 + SUBLANES, SUBLANES), im] = xi
        return carry

    lax.fori_loop(0, n_tiles, tile_body, 0)

    for b in range(nb):
        ys = [_dot(h_s[b, m, SUBLANES:SUBLANES + tc, :], cm_ref[m], False) for m in range(n_units)]
        y = jnp.concatenate(ys, axis=1) + d_ref[...] * u_ref[b]
        o_ref[b] = _gelu_glu(y, wg_ref, bg_ref, False)
        for m in range(n_units):
            last = h_s[b, m, tc + SUBLANES - 1:tc + SUBLANES, :]
            hre_ref[b, :, m * sw:(m + 1) * sw] = last[:, :sw]
            him_ref[b, :, m * sw:(m + 1) * sw] = last[:, sw:]


def _ssm_prompt(u, w_units, c_units, tab, d_flat, w_glu_bf, b_glu):
    b, t, width = u.shape
    tc = min(SSM_ROWS, t)
    n_units, uw, sw2 = w_units.shape
    states = tab.shape[-1]
    const = lambda shape: pl.BlockSpec(shape, lambda i: (0,) * len(shape))
    return pl.pallas_call(
        _ssm_p_kernel,
        grid=(t // tc,),
        in_specs=[pl.BlockSpec((b, tc, width), lambda i: (0, i, 0)),
                  const(w_units.shape), const(c_units.shape), const(tab.shape),
                  const((1, width)), const(w_glu_bf.shape), const((1, width))],
        out_specs=(pl.BlockSpec((b, tc, width), lambda i: (0, i, 0)),
                   const((b, 1, states)), const((b, 1, states))),
        out_shape=(jax.ShapeDtypeStruct((b, t, width), F32),
                   jax.ShapeDtypeStruct((b, 1, states), F32),
                   jax.ShapeDtypeStruct((b, 1, states), F32)),
        scratch_shapes=[pltpu.VMEM((b, n_units, tc, sw2), F32),
                        pltpu.VMEM((b, n_units, tc + SUBLANES, sw2), F32)],
        compiler_params=_cparams(1),
        name="ssm_prompt",
    )(u, w_units, c_units, tab, d_flat, w_glu_bf, b_glu)


def _ssm_s_kernel(u_ref, h0r_ref, h0i_ref, lam_ref, wre_ref, wim_ref, cre_ref, cim_ref,
                  d_ref, wg_ref, bg_ref, o_ref, hre_ref, him_ref):
    u = u_ref[...]
    lr, li = lam_ref[0:1, :], lam_ref[1:2, :]
    h0r, h0i = h0r_ref[...], h0i_ref[...]
    hr = (lr * h0r - li * h0i) + _dot(u, wre_ref[...], True)
    hi = (lr * h0i + li * h0r) + _dot(u, wim_ref[...], True)
    hre_ref[...] = hr
    him_ref[...] = hi
    y = _dot(hr, cre_ref[...], True) - _dot(hi, cim_ref[...], True) + d_ref[...] * u
    o_ref[...] = _gelu_glu(y, wg_ref, bg_ref, True)


def _ssm_sample(u, h0r, h0i, lam2, wre, wim, cre, cim, d_flat, w_glu, b_glu):
    rows, width = u.shape
    states = h0r.shape[1]
    args = (u, h0r, h0i, lam2, wre, wim, cre, cim, d_flat, w_glu, b_glu)
    full = lambda a: pl.BlockSpec(a.shape, lambda i: (0,) * a.ndim)
    return pl.pallas_call(
        _ssm_s_kernel,
        grid=(1,),
        in_specs=[full(a) for a in args],
        out_specs=(pl.BlockSpec((rows, width), lambda i: (0, 0)),
                   pl.BlockSpec((rows, states), lambda i: (0, 0)),
                   pl.BlockSpec((rows, states), lambda i: (0, 0))),
        out_shape=(jax.ShapeDtypeStruct((rows, width), F32),
                   jax.ShapeDtypeStruct((rows, states), F32),
                   jax.ShapeDtypeStruct((rows, states), F32)),
        compiler_params=_cparams(1),
        name="ssm_sample",
    )(*args)


def _post_kernel(x_ref, oa_ref, os_ref, g1_ref, sh2_ref, sc2_ref, g2_ref,
                 ga_ref, gs_ref, gpm_ref, gpf_ref, gff_ref, wo_ref, wu_ref, wd_ref, y_ref,
                 *, precise):
    merged = jnp.concatenate([_rms(oa_ref[...], ga_ref[...]), _rms(os_ref[...], gs_ref[...])],
                             axis=-1)
    x1 = x_ref[...] + g1_ref[...] * _rms(_dot(merged, wo_ref[...], precise), gpm_ref[...])
    h = _rms(x1, gpf_ref[...]) * (1.0 + sc2_ref[...]) + sh2_ref[...]
    up = jnp.maximum(_dot(h, wu_ref[...], precise), 0.0)
    f = _dot(up * up, wd_ref[...], precise)
    y_ref[...] = x1 + g2_ref[...] * _rms(f, gff_ref[...])


def _post_prompt(x, o_attn, o_ssm, mod, gains, w_out, w_up, w_down):
    b, t, d = x.shape
    aw = o_attn.shape[-1]
    tb = min(PROJ_ROWS, t)
    row = lambda bi, i: (bi, i, 0)
    modspec = lambda which: pl.BlockSpec((None, None, 1, d), lambda bi, i: (bi, which, 0, 0))
    const = lambda a: pl.BlockSpec(a.shape, lambda bi, i: (0,) * a.ndim,
                                   pipeline_mode=pl.Buffered(1))
    return pl.pallas_call(
        functools.partial(_post_kernel, precise=False),
        grid=(b, t // tb),
        in_specs=[pl.BlockSpec((None, tb, d), row),
                  pl.BlockSpec((None, tb, aw), row), pl.BlockSpec((None, tb, aw), row),
                  modspec(2), modspec(3), modspec(4), modspec(5)]
                 + [const(g) for g in gains] + [const(w_out), const(w_up), const(w_down)],
        out_specs=pl.BlockSpec((None, tb, d), row),
        out_shape=jax.ShapeDtypeStruct((b, t, d), F32),
        compiler_params=_cparams(2),
        name="post_prompt",
    )(x, o_attn, o_ssm, mod, mod, mod, mod, *gains, w_out, w_up, w_down)


def _post_sample(x, o_attn, o_ssm, g1, sh2, sc2, g2, gains, w_out, w_up, w_down):
    rows, d = x.shape
    ff = w_up.shape[1]
    args = (x, o_attn, o_ssm, g1, sh2, sc2, g2, *gains, w_out, w_up, w_down)
    full = lambda a: pl.BlockSpec(a.shape, lambda i: (0,) * a.ndim, pipeline_mode=pl.Buffered(1))
    return pl.pallas_call(
        functools.partial(_post_kernel, precise=False),
        grid=(1,),
        in_specs=[full(a) for a in args],
        out_specs=pl.BlockSpec((rows, d), lambda i: (0, 0)),
        out_shape=jax.ShapeDtypeStruct((rows, d), F32),
        compiler_params=_cparams(1),
        name="post_sample",
    )(*args)


def _block_diag(blocks):
    g, r, c = blocks.shape
    eye = jnp.eye(g, dtype=blocks.dtype)
    return (blocks[:, :, None, :] * eye[:, None, :, None]).reshape(g * r, g * c)


def _layer(xp, xs, c_prompt, c_sample, pool_k, pool_v, h0_re, h0_im, page_table, p):
    b, t, d = xp.shape
    db = xs.shape[0]
    g, pstate = p["ssm_lam_re"].shape
    n_heads = p["sb_bias"].shape[0]
    attn_w = n_heads * HEAD_DIM
    ssm_w = g * SSM_GROUP
    states = g * pstate

    bbre, bbim, tab = _prep(p["ssm_lam_re"], p["ssm_lam_im"], p["ssm_log_dt"],
                            p["ssm_b_re"], p["ssm_b_im"])
    wre = _block_diag(bbre)
    wim = _block_diag(bbim)
    cre = _block_diag(p["ssm_c_re"].transpose(0, 2, 1))
    cim = _block_diag(p["ssm_c_im"].transpose(0, 2, 1))
    n_units = g // SSM_UNIT
    uw, sw = SSM_UNIT * SSM_GROUP, SSM_UNIT * pstate
    w_units = jnp.stack([jnp.concatenate([wre[m * uw:(m + 1) * uw, m * sw:(m + 1) * sw],
                                          wim[m * uw:(m + 1) * uw, m * sw:(m + 1) * sw]], axis=1)
                         for m in range(n_units)]).astype(BF16)
    c_units = jnp.stack([jnp.concatenate([cre[m * sw:(m + 1) * sw, m * uw:(m + 1) * uw],
                                          -cim[m * sw:(m + 1) * sw, m * uw:(m + 1) * uw]], axis=0)
                         for m in range(n_units)]).astype(BF16)
    d_flat = p["ssm_d"].reshape(1, ssm_w)
    b_glu = p["b_glu"].reshape(1, ssm_w)
    bias2 = p["sb_bias"] * LOG2E
    q_scale = HEAD_DIM ** -0.5 * LOG2E
    row1 = lambda a: a.reshape(1, -1)
    gains = [row1(p[n]) for n in ("g_attn_out", "g_ssm_out", "g_post_mix", "g_pre_ffn",
                                  "g_post_ffn")]

    n_mod = b + db
    pad = (-n_mod) % SUBLANES
    c_all = jnp.concatenate([c_prompt, c_sample, jnp.zeros((pad, d), F32)], axis=0)
    mod = _ada(c_all, p["w_ada"], p["b_ada"])
    mod_p = mod[:b].reshape(b, 6, 1, d)
    mod_s = mod[b:b + db].reshape(db, 6, d)

    w_in_bf = p["w_in"].astype(BF16)
    q2, k, v, u, kb, vb = _proj_prompt(xp, mod_p, row1(p["g_pre_mix"]), w_in_bf, q_scale)
    ot = _attn_prompt(bias2, q2.transpose(0, 2, 1), kb, vb)
    o_attn = ot.transpose(0, 2, 1)
    o_ssm, hre_p, him_p = _ssm_prompt(u, w_units, c_units, tab, d_flat,
                                      p["w_glu"].astype(BF16), b_glu)
    w_post = [p[n].astype(BF16) for n in ("w_out", "w_up", "w_down")]
    yp = _post_prompt(xp, o_attn, o_ssm, mod_p, gains, *w_post)

    xs2 = xs.reshape(db, d)
    qs, ks, vs, us = _proj_sample(xs2, mod_s[:, 0], mod_s[:, 1], row1(p["g_pre_mix"]),
                                  p["w_in"], q_scale)
    n_pool, page = pool_k.shape[0], pool_k.shape[1]
    oa_s = _attn_sample(page_table, qs, ks, vs, bias2,
                        pool_k.reshape(n_pool, page, attn_w),
                        pool_v.reshape(n_pool, page, attn_w), n_heads)
    lam2 = jnp.concatenate([tab[6, 0:1], tab[7, 0:1]], axis=0)
    os_s, hre_s, him_s = _ssm_sample(us, h0_re.reshape(db, states), h0_im.reshape(db, states),
                                     lam2, wre, wim, cre, cim, d_flat, p["w_glu"], b_glu)
    ys = _post_sample(xs2, oa_s, os_s, mod_s[:, 2], mod_s[:, 3], mod_s[:, 4], mod_s[:, 5],
                      gains, *w_post)

    heads = (n_heads, HEAD_DIM)
    return (yp, ys.reshape(db, 1, d),
            k.reshape(b, t, *heads), v.reshape(b, t, *heads),
            hre_p.reshape(b, g, pstate), him_p.reshape(b, g, pstate),
            ks.reshape(db, 1, *heads), vs.reshape(db, 1, *heads),
            hre_s.reshape(db, g, pstate), him_s.reshape(db, g, pstate))


def kernel(x_prompt, x_sample, c_prompt, c_sample, cache_k, cache_v, state_ssm_re, state_ssm_im, page_table, w_ada, b_ada, g_pre_mix, w_in, sb_bias, ssm_lam_re, ssm_lam_im, ssm_log_dt, ssm_b_re, ssm_b_im, ssm_c_re, ssm_c_im, ssm_d, w_glu, b_glu, g_attn_out, g_ssm_out, w_out, g_post_mix, g_pre_ffn, w_up, w_down, g_post_ffn):
    weights = dict(w_ada=w_ada, b_ada=b_ada, g_pre_mix=g_pre_mix, w_in=w_in, sb_bias=sb_bias,
                   ssm_lam_re=ssm_lam_re, ssm_lam_im=ssm_lam_im, ssm_log_dt=ssm_log_dt,
                   ssm_b_re=ssm_b_re, ssm_b_im=ssm_b_im, ssm_c_re=ssm_c_re, ssm_c_im=ssm_c_im,
                   ssm_d=ssm_d, w_glu=w_glu, b_glu=b_glu, g_attn_out=g_attn_out,
                   g_ssm_out=g_ssm_out, w_out=w_out, g_post_mix=g_post_mix,
                   g_pre_ffn=g_pre_ffn, w_up=w_up, w_down=w_down, g_post_ffn=g_post_ffn)
    depth = w_in.shape[0]
    assert x_sample.shape[1] == 1, "decode path handles one new token per sequence"
    xp, xs = x_prompt, x_sample
    outs = []
    for l in range(depth):
        p = {n: a[l] for n, a in weights.items()}
        res = _layer(xp, xs, c_prompt, c_sample, cache_k[l], cache_v[l],
                     state_ssm_re[l], state_ssm_im[l], page_table, p)
        xp, xs = res[0], res[1]
        outs.append(res[2:])
    stacked = tuple(jnp.stack([o[i] for o in outs]) for i in range(8))
    return (xp, xs) + stacked
```

```python
import functools
import math

import numpy as np
import jax
import jax.numpy as jnp
from jax import lax
from jax.experimental import pallas as pl
from jax.experimental.pallas import tpu as pltpu

F32 = jnp.float32
BF16 = jnp.bfloat16
HIGHEST = lax.Precision.HIGHEST

RMS_EPS = 1e-6
HEAD_DIM = 64
SSM_GROUP = 16
SSM_STATE = 64
LOG2E = 1.4426950408889634
LN2 = 0.6931471805599453

LANES = 128
SUBLANES = 8
KEY_BLOCK = 128
Q_WIDTH = 512
PROJ_ROWS = 512
SSM_ROWS = 256
PAGES_PER_STEP = 8
SSM_UNIT = 16
VMEM_LIMIT = 56 * 1024 * 1024


def _cparams(n_axes):
    return pltpu.CompilerParams(dimension_semantics=("arbitrary",) * n_axes,
                                vmem_limit_bytes=VMEM_LIMIT)


def _rms(x, g):
    inv = lax.rsqrt(jnp.mean(x * x, axis=-1, keepdims=True) + RMS_EPS)
    return (x * inv) * g


def _dot(a, b, precise):
    if precise:
        return jnp.dot(a, b, precision=HIGHEST, preferred_element_type=F32)
    return jnp.dot(a.astype(BF16), b.astype(BF16), preferred_element_type=F32)


def _softplus2(z2):
    neg_abs = lax.bitcast_convert_type(
        lax.bitcast_convert_type(z2, jnp.uint32) | jnp.uint32(0x80000000), F32)
    e = jnp.exp2(neg_abs)
    return jnp.maximum(z2, 0.0) + jnp.log(1.0 + e) * LOG2E


def _split_bf16(p):
    hi = lax.bitcast_convert_type(
        lax.bitcast_convert_type(p, jnp.uint32) & jnp.uint32(0xFFFF0000), F32)
    return hi.astype(BF16), (p - hi).astype(BF16)


def _prep_kernel(lre_ref, lim_ref, dt_ref, bre_ref, bim_ref, lre8_ref, lim8_ref, dt8_ref,
                 bbre_ref, bbim_ref, tab_ref):
    def lam_bar(lre, lim, log_dt):
        dt = jnp.exp(log_dt)
        mag = jnp.exp(lre * dt)
        ang = lim * dt
        return mag * jnp.cos(ang), mag * jnp.sin(ang)

    lre, lim = lre_ref[...], lim_ref[...]
    lbr, lbi = lam_bar(lre, lim, dt_ref[...])
    nr, ni = lbr - 1.0, lbi
    den = lre * lre + lim * lim
    cre = (nr * lre + ni * lim) / den
    cim = (ni * lre - nr * lim) / den
    bre, bim = bre_ref[...], bim_ref[...]
    bbre_ref[...] = cre * bre - cim * bim
    bbim_ref[...] = cre * bim + cim * bre

    p1r, p1i = lam_bar(lre8_ref[...], lim8_ref[...], dt8_ref[...])
    row = lax.broadcasted_iota(jnp.int32, p1r.shape, 0)

    def cmul(ar, ai, br, bi):
        return ar * br - ai * bi, ar * bi + ai * br

    p2r, p2i = cmul(p1r, p1i, p1r, p1i)
    p4r, p4i = cmul(p2r, p2i, p2r, p2i)
    p8r, p8i = cmul(p4r, p4i, p4r, p4i)
    e = row + 1
    accr, acci = jnp.ones_like(p1r), jnp.zeros_like(p1r)
    for bit, (pr, pi) in enumerate(((p1r, p1i), (p2r, p2i), (p4r, p4i), (p8r, p8i))):
        on = ((e >> bit) & 1) == 1
        nr_, ni_ = cmul(accr, acci, pr, pi)
        accr = jnp.where(on, nr_, accr)
        acci = jnp.where(on, ni_, acci)
    zero = jnp.zeros_like(p1r)
    tab_ref[0] = jnp.where(row >= 1, p1r, zero)
    tab_ref[1] = jnp.where(row >= 1, p1i, zero)
    tab_ref[2] = jnp.where(row >= 2, p2r, zero)
    tab_ref[3] = jnp.where(row >= 2, p2i, zero)
    tab_ref[4] = jnp.where(row >= 4, p4r, zero)
    tab_ref[5] = jnp.where(row >= 4, p4i, zero)
    tab_ref[6] = accr
    tab_ref[7] = acci


def _prep(lam_re, lam_im, log_dt, b_re, b_im):
    g, p = lam_re.shape
    c = b_re.shape[-1]
    rows = g * c

    def rep(a):
        return jnp.broadcast_to(a[:, None, :], (g, c, p)).reshape(rows, p)

    def flat8(a):
        return jnp.broadcast_to(a.reshape(1, g * p), (SUBLANES, g * p))

    dt_gp = jnp.broadcast_to(log_dt[:, None], (g, p))
    bt = lambda b: b.transpose(0, 2, 1).reshape(rows, p)
    bbre, bbim, tab = pl.pallas_call(
        _prep_kernel,
        out_shape=(jax.ShapeDtypeStruct((rows, p), F32),
                   jax.ShapeDtypeStruct((rows, p), F32),
                   jax.ShapeDtypeStruct((8, SUBLANES, g * p), F32)),
        name="prep",
    )(rep(lam_re), rep(lam_im), rep(dt_gp), bt(b_re), bt(b_im),
      flat8(lam_re), flat8(lam_im), flat8(dt_gp))
    return bbre.reshape(g, c, p), bbim.reshape(g, c, p), tab


def _ada_kernel(c_ref, w_ref, b_ref, o_ref):
    c = c_ref[...]
    s = c * jax.nn.sigmoid(c)
    o_ref[...] = _dot(s, w_ref[...], True) + b_ref[...]


def _ada(c_all, w_ada, b_ada):
    rows, d = c_all.shape
    n = w_ada.shape[1]
    bn = 1024
    return pl.pallas_call(
        _ada_kernel,
        grid=(n // bn,),
        in_specs=[pl.BlockSpec((rows, d), lambda j: (0, 0)),
                  pl.BlockSpec((d, bn), lambda j: (0, j)),
                  pl.BlockSpec((1, bn), lambda j: (0, j))],
        out_specs=pl.BlockSpec((rows, bn), lambda j: (0, j)),
        out_shape=jax.ShapeDtypeStruct((rows, n), F32),
        compiler_params=_cparams(1),
        name="ada",
    )(c_all, w_ada, b_ada.reshape(1, n))


def _proj_kernel(x_ref, sh_ref, sc_ref, g_ref, w_ref, *out_refs, attn_w, precise, q_scale):
    h = _rms(x_ref[...], g_ref[...]) * (1.0 + sc_ref[...]) + sh_ref[...]
    p = _dot(h, w_ref[...], precise)
    q, k, v, u = (p[:, i * attn_w:(i + 1) * attn_w] for i in range(4))
    if precise:
        q_ref, k_ref, v_ref, u_ref = out_refs
        q_ref[...] = q * q_scale
    else:
        q_ref, k_ref, v_ref, u_ref, kb_ref, vb_ref = out_refs
        q_ref[...] = (q * q_scale).astype(BF16)
        kb_ref[...] = k.astype(BF16)
        vb_ref[...] = v.astype(BF16)
    k_ref[...] = k
    v_ref[...] = v
    u_ref[...] = u


def _proj_prompt(x, mod, g_pre, w_in_bf, q_scale):
    b, t, d = x.shape
    n = w_in_bf.shape[1]
    aw = n // 4
    tb = min(PROJ_ROWS, t)
    row = lambda bi, i: (bi, i, 0)
    modspec = lambda which: pl.BlockSpec((None, None, 1, d), lambda bi, i: (bi, which, 0, 0))
    f32o = jax.ShapeDtypeStruct((b, t, aw), F32)
    bfo = jax.ShapeDtypeStruct((b, t, aw), BF16)
    ospec = pl.BlockSpec((None, tb, aw), row)
    return pl.pallas_call(
        functools.partial(_proj_kernel, attn_w=aw, precise=False, q_scale=q_scale),
        grid=(b, t // tb),
        in_specs=[pl.BlockSpec((None, tb, d), row), modspec(0), modspec(1),
                  pl.BlockSpec((1, d), lambda bi, i: (0, 0)),
                  pl.BlockSpec((d, n), lambda bi, i: (0, 0))],
        out_specs=(ospec,) * 6,
        out_shape=(bfo, f32o, f32o, f32o, bfo, bfo),
        compiler_params=_cparams(2),
        name="proj_prompt",
    )(x, mod, mod, g_pre, w_in_bf)


def _proj_sample(x, sh, sc, g_pre, w_in, q_scale):
    rows, d = x.shape
    n = w_in.shape[1]
    aw = n // 4
    full = lambda shape: pl.BlockSpec(shape, lambda i: (0,) * len(shape))
    o = jax.ShapeDtypeStruct((rows, aw), F32)
    return pl.pallas_call(
        functools.partial(_proj_kernel, attn_w=aw, precise=True, q_scale=q_scale),
        grid=(1,),
        in_specs=[full((rows, d)), full((rows, d)), full((rows, d)), full((1, d)), full((d, n))],
        out_specs=(full((rows, aw)),) * 4,
        out_shape=(o, o, o, o),
        compiler_params=_cparams(1),
        name="proj_sample",
    )(x, sh, sc, g_pre, w_in)


BF16_TILE_ROWS = 16
NU_DIM = KEY_BLOCK + BF16_TILE_ROWS
CARRY_ROWS = 3
MASKED = -1e30
PIPE = 4


def _suffix_matrix():
    s = np.arange(NU_DIM)[:, None]
    j = np.arange(NU_DIM)[None, :]
    key = j < KEY_BLOCK
    m = np.where(key & ((j >= s) | (s >= KEY_BLOCK)), -1.0, 0.0)
    m = m + np.where((s < KEY_BLOCK) & (j >= KEY_BLOCK) & (j < KEY_BLOCK + CARRY_ROWS), 1.0, 0.0)
    return jnp.asarray(m, dtype=BF16)


def _attn_p_kernel(bias_ref, qt_ref, k_ref, vt_ref, nu_ref, o_ref,
                   qm_s, bm_s, z_s, p_s, r_s, a_s, acc_s, carry_s):
    hp = pl.program_id(1)
    qi = pl.program_id(2)
    qw = qt_ref.shape[1]
    n_chunks = qw // LANES
    bpq = qw // KEY_BLOCK
    zeros = lambda ref, *idx: jnp.zeros(ref.shape[len(idx):], ref.dtype)

    @pl.when(qi == 0)
    def _():
        s_io = lax.broadcasted_iota(jnp.int32, (KEY_BLOCK, qw), 0)
        t_io = lax.broadcasted_iota(jnp.int32, (KEY_BLOCK, qw), 1)
        for hh in range(2):
            bias2 = bias_ref[2 * hp + hh]
            bm_s[hh, 0] = jnp.full((KEY_BLOCK, qw), bias2, F32)
            for rel in range(bpq):
                vis = s_io + rel * KEY_BLOCK < t_io
                bm_s[hh, rel + 1] = jnp.where(vis, bias2, MASKED)

    q32 = qt_ref[...].astype(F32)
    row = lax.broadcasted_iota(jnp.int32, q32.shape, 0)
    for hh in range(2):
        own = (row >= hh * HEAD_DIM) & (row < (hh + 1) * HEAD_DIM)
        qm_s[hh] = jnp.where(own, q32, 0.0).astype(BF16)
        acc_s[hh] = zeros(acc_s, hh)
        carry_s[hh] = zeros(carry_s, hh)
        for slot in range(1, PIPE):
            z_s[hh, slot] = jnp.full(z_s.shape[2:], MASKED, F32)
        p_s[hh, 0] = zeros(p_s, hh, 0)
        r_s[hh, 1] = zeros(r_s, hh, 1)
        a_s[hh, 0] = zeros(a_s, hh, 0)

    last = (qi + 1) * bpq - 1
    row8 = lax.broadcasted_iota(jnp.int32, (SUBLANES, qw), 0)
    top16 = lambda x: lax.bitcast_convert_type(
        lax.bitcast_convert_type(x, jnp.uint32) & jnp.uint32(0xFFFF0000), F32)

    def stage(i, u):
        j_in = jnp.maximum(last - i, 0)
        k_in = k_ref[pl.ds(pl.multiple_of(j_in * KEY_BLOCK, KEY_BLOCK), KEY_BLOCK), :]
        m_in = jnp.clip(j_in - qi * bpq, -1, bpq - 1) + 1
        j_out = jnp.clip(last - (i - PIPE), 0, last)
        for hh in range(2):
            z_s[hh, u] = jnp.dot(k_in, qm_s[hh], preferred_element_type=F32) + bm_s[hh, m_in]
        for hh in range(2):
            carry = carry_s[hh] + r_s[hh, (u + 1) % 2, KEY_BLOCK:KEY_BLOCK + SUBLANES, :]
            carry_s[hh] = carry
            c_hi = top16(carry)
            c_mid = top16(carry - c_hi)
            c_lo = (carry - c_hi) - c_mid
            addends = jnp.where(row8 == 0, c_hi, jnp.where(row8 == 1, c_mid,
                                                           jnp.where(row8 == 2, c_lo, 0.0)))
            p_s[hh, u % 2, KEY_BLOCK:NU_DIM, :] = jnp.concatenate(
                [addends, jnp.zeros_like(addends)], axis=0).astype(BF16)
            r_s[hh, u % 2] = jnp.dot(nu_ref[...], p_s[hh, u % 2], preferred_element_type=F32)
        for hh in range(2):
            vt = vt_ref[j_out, hh * HEAD_DIM:(hh + 1) * HEAD_DIM, :]
            acc_s[hh] += jnp.dot(vt, a_s[hh, u % 2], preferred_element_type=F32)
        for hh in range(2):
            for c in range(n_chunks):
                sl = slice(c * LANES, (c + 1) * LANES)
                p = _softplus2(z_s[hh, (u - 1) % PIPE, :, sl])
                p_s[hh, (u + 1) % 2, 0:KEY_BLOCK, sl] = p.astype(BF16)
        for hh in range(2):
            for c in range(n_chunks):
                sl = slice(c * LANES, (c + 1) * LANES)
                log_a = z_s[hh, (u + 1) % PIPE, :, sl] + r_s[hh, (u + 1) % 2, 0:KEY_BLOCK, sl]
                a_s[hh, (u + 1) % 2, :, sl] = jnp.exp2(log_a).astype(BF16)

    def body(n, c):
        for u in range(PIPE):
            stage(n * PIPE + u, u)
        return c

    lax.fori_loop(0, qi + 2, body, 0)

    for hh in range(2):
        o_ref[hh * HEAD_DIM:(hh + 1) * HEAD_DIM, :] = acc_s[hh]


def _attn_prompt(bias2, qt, kb, vb):
    b, w, t = qt.shape
    qw = Q_WIDTH
    assert t % qw == 0 and qw // KEY_BLOCK == PIPE
    pair = 2 * HEAD_DIM
    nkb = t // KEY_BLOCK
    vt = vb.reshape(b, nkb, KEY_BLOCK, w).transpose(0, 1, 3, 2)
    return pl.pallas_call(
        _attn_p_kernel,
        grid=(b, w // pair, t // qw),
        in_specs=[pl.BlockSpec(memory_space=pltpu.SMEM),
                  pl.BlockSpec((None, pair, qw), lambda bi, hp, qi: (bi, hp, qi)),
                  pl.BlockSpec((None, t, pair), lambda bi, hp, qi: (bi, 0, hp)),
                  pl.BlockSpec((None, nkb, pair, KEY_BLOCK), lambda bi, hp, qi: (bi, 0, hp, 0)),
                  pl.BlockSpec((NU_DIM, NU_DIM), lambda bi, hp, qi: (0, 0))],
        out_specs=pl.BlockSpec((None, pair, qw), lambda bi, hp, qi: (bi, hp, qi)),
        out_shape=jax.ShapeDtypeStruct((b, w, t), F32),
        scratch_shapes=[pltpu.VMEM((2, pair, qw), BF16),
                        pltpu.VMEM((2, PIPE + 1, KEY_BLOCK, qw), F32),
                        pltpu.VMEM((2, PIPE, KEY_BLOCK, qw), F32),
                        pltpu.VMEM((2, 2, NU_DIM, qw), BF16),
                        pltpu.VMEM((2, 2, NU_DIM, qw), F32),
                        pltpu.VMEM((2, 2, KEY_BLOCK, qw), BF16),
                        pltpu.VMEM((2, HEAD_DIM, qw), F32),
                        pltpu.VMEM((2, SUBLANES, qw), F32)],
        compiler_params=_cparams(3),
        name="attn_prompt",
    )(bias2, qt, kb, vt, _suffix_matrix())


def _neg_suffix_matrix_lanes():
    j = np.arange(2 * KEY_BLOCK)[:, None] % KEY_BLOCK
    s = np.arange(2 * KEY_BLOCK)[None, :]
    return jnp.asarray(np.where((j >= s) | (s >= KEY_BLOCK), -1.0, 0.0), dtype=BF16)


def _attn_s_kernel(pt_ref, q_ref, kn_ref, vn_ref, bias_ref, nu_ref, *refs):
    pages = PAGES_PER_STEP
    k_refs, v_refs = refs[:pages], refs[pages:2 * pages]
    o_ref = refs[2 * pages]
    acc_s, carry_s = refs[2 * pages + 1:]
    g = pl.program_id(1)
    heads = kn_ref.shape[1]
    span = pages * KEY_BLOCK
    bias = bias_ref[...]
    q16 = q_ref[0]

    @pl.when(g == 0)
    def _():
        tpos = lax.broadcasted_iota(jnp.int32, (heads, 1), 1)
        vis = tpos < tpos
        z_new = jnp.sum(q16[0:heads] * kn_ref[0], axis=-1, keepdims=True) + bias
        sp = _softplus2(z_new)
        log_1m = jnp.where(vis, -sp, 0.0)
        a_new = jnp.where(vis, jnp.exp2(z_new - sp), 0.0)
        acc_s[...] = a_new * vn_ref[0]
        carry_s[...] = jnp.broadcast_to(log_1m, carry_s.shape)

    def head_rows(page_refs, h):
        return jnp.concatenate(
            [ref[0, pl.ds(h, KEY_BLOCK, stride=heads), :].astype(BF16) for ref in page_refs],
            axis=0)

    qb = q16.astype(BF16)
    row = lax.broadcasted_iota(jnp.int32, (heads, span), 0)
    z = jnp.zeros((heads, span), F32)
    for h in range(heads):
        zh = lax.dot_general(qb, head_rows(k_refs, h), (((1,), (1,)), ((), ())),
                             preferred_element_type=F32)
        z = jnp.where(row == h, zh[0:heads], z)
    z = z + bias
    sp = jnp.concatenate([_softplus2(z[:, r * KEY_BLOCK:(r + 1) * KEY_BLOCK])
                          for r in range(pages)], axis=0)
    hi, lo = _split_bf16(sp)
    rr = jnp.dot(jnp.concatenate([hi, lo], axis=1), nu_ref[...], preferred_element_type=F32)
    carry = carry_s[...]
    weights = []
    for r in range(pages):
        blk = rr[r * heads:(r + 1) * heads]
        weights.append(jnp.exp2(z[:, r * KEY_BLOCK:(r + 1) * KEY_BLOCK] + blk[:, :KEY_BLOCK]
                                + carry))
        carry = carry + blk[:, KEY_BLOCK:]
    carry_s[...] = carry
    a = jnp.concatenate(weights, axis=1)
    acc = acc_s[...]
    for h in range(heads):
        ah = jnp.where(row == h, a, 0.0)
        ah = jnp.concatenate([ah, jnp.zeros_like(ah)], axis=0).astype(BF16)
        acc = acc + jnp.dot(ah, head_rows(v_refs, h), preferred_element_type=F32)[0:heads]
    acc_s[...] = acc

    @pl.when(g == pl.num_programs(1) - 1)
    def _():
        o_ref[0] = acc_s[...]


def _attn_sample(page_table, q2, k_new, v_new, bias2, pool_k, pool_v):
    db, heads, dh = q2.shape
    n_pool, page_rows = pool_k.shape[:2]
    n_pages = page_table.shape[1]
    pages = PAGES_PER_STEP
    assert n_pages % pages == 0 and page_rows == KEY_BLOCK and heads <= SUBLANES
    q16 = jnp.concatenate([q2, jnp.zeros((db, BF16_TILE_ROWS - heads, dh), F32)], axis=1)
    tokspec = lambda rows: pl.BlockSpec((1, rows, dh), lambda s, g, pt: (s, 0, 0))
    flat = lambda pool: pool.reshape(n_pool, page_rows * heads, dh)

    def page_spec(r):
        return pl.BlockSpec((1, page_rows * heads, dh),
                            lambda s, g, pt: (pt[s, n_pages - 1 - (g * pages + r)], 0, 0))

    grid_spec = pltpu.PrefetchScalarGridSpec(
        num_scalar_prefetch=1,
        grid=(db, n_pages // pages),
        in_specs=[tokspec(BF16_TILE_ROWS), tokspec(heads), tokspec(heads),
                  pl.BlockSpec((heads, 1), lambda s, g, pt: (0, 0)),
                  pl.BlockSpec((2 * KEY_BLOCK, 2 * KEY_BLOCK), lambda s, g, pt: (0, 0))]
                 + [page_spec(r) for r in range(pages)] * 2,
        out_specs=tokspec(heads),
        scratch_shapes=[pltpu.VMEM((heads, dh), F32),
                        pltpu.VMEM((heads, KEY_BLOCK), F32)])
    return pl.pallas_call(
        _attn_s_kernel,
        grid_spec=grid_spec,
        out_shape=jax.ShapeDtypeStruct((db, heads, dh), F32),
        compiler_params=_cparams(2),
        name="attn_sample",
    )(page_table, q16, k_new, v_new, bias2.reshape(heads, 1), _neg_suffix_matrix_lanes(),
      *([flat(pool_k)] * pages), *([flat(pool_v)] * pages))


def _gelu_glu(y, wg_ref, bg_ref, precise):
    g = 0.5 * y * (1.0 + jnp.tanh(math.sqrt(2.0 / math.pi) * (y + 0.044715 * (y * y * y))))
    return g * jax.nn.sigmoid(_dot(g, wg_ref[...], precise) + bg_ref[...])


def _ssm_p_kernel(u_ref, w_ref, cm_ref, tab_ref, d_ref, wg_ref, bg_ref,
                  o_ref, hre_ref, him_ref, bu_s, h_s):
    step = pl.program_id(0)
    nb, tc, width = u_ref.shape
    n_units = w_ref.shape[0]
    uw = width // n_units
    sw = w_ref.shape[2] // 2
    n_tiles = tc // SUBLANES

    @pl.when(step == 0)
    def _():
        h_s[:, :, 0:SUBLANES, :] = jnp.zeros((nb, n_units, SUBLANES, 2 * sw), F32)

    @pl.when(step > 0)
    def _():
        h_s[:, :, 0:SUBLANES, :] = h_s[:, :, tc:tc + SUBLANES, :]

    for b in range(nb):
        for m in range(n_units):
            bu_s[b, m] = _dot(u_ref[b, :, m * uw:(m + 1) * uw], w_ref[m], False)

    def tile_body(tb, carry):
        r0 = pl.multiple_of(tb * SUBLANES, SUBLANES)
        for b in range(nb):
            for m in range(n_units):
                for c in range(sw // LANES):
                    re = slice(c * LANES, (c + 1) * LANES)
                    im = slice(sw + c * LANES, sw + (c + 1) * LANES)
                    tl = slice(m * sw + c * LANES, m * sw + (c + 1) * LANES)
                    xr = bu_s[b, m, pl.ds(r0, SUBLANES), re]
                    xi = bu_s[b, m, pl.ds(r0, SUBLANES), im]
                    for k, ti in ((1, 0), (2, 2), (4, 4)):
                        ar, ai = tab_ref[ti, :, tl], tab_ref[ti + 1, :, tl]
                        sr = pltpu.roll(xr, k, 0)
                        si = pltpu.roll(xi, k, 0)
                        xr, xi = xr + (ar * sr - ai * si), xi + (ar * si + ai * sr)
                    prev_r = h_s[b, m, pl.ds(r0, SUBLANES), re]
                    prev_i = h_s[b, m, pl.ds(r0, SUBLANES), im]
                    pr = jnp.broadcast_to(prev_r[SUBLANES - 1:SUBLANES, :], (SUBLANES, LANES))
                    pi = jnp.broadcast_to(prev_i[SUBLANES - 1:SUBLANES, :], (SUBLANES, LANES))
                    lr, li = tab_ref[6, :, tl], tab_ref[7, :, tl]
                    xr, xi = xr + (lr * pr - li * pi), xi + (lr * pi + li * pr)
                    h_s[b, m, pl.ds(r0 + SUBLANES, SUBLANES), re] = xr
                    h_s[b, m, pl.ds(r0 + SUBLANES, SUBLANES), im] = xi
        return carry

    lax.fori_loop(0, n_tiles, tile_body, 0)

    for b in range(nb):
        ys = [_dot(h_s[b, m, SUBLANES:SUBLANES + tc, :], cm_ref[m], False) for m in range(n_units)]
        y = jnp.concatenate(ys, axis=1) + d_ref[...] * u_ref[b]
        o_ref[b] = _gelu_glu(y, wg_ref, bg_ref, False)
        for m in range(n_units):
            last = h_s[b, m, tc + SUBLANES - 1:tc + SUBLANES, :]
            hre_ref[b, :, m * sw:(m + 1) * sw] = last[:, :sw]
            him_ref[b, :, m * sw:(m + 1) * sw] = last[:, sw:]


def _ssm_prompt(u, w_units, c_units, tab, d_flat, w_glu_bf, b_glu):
    b, t, width = u.shape
    tc = min(SSM_ROWS, t)
    n_units, uw, sw2 = w_units.shape
    states = tab.shape[-1]
    const = lambda shape: pl.BlockSpec(shape, lambda i: (0,) * len(shape))
    return pl.pallas_call(
        _ssm_p_kernel,
        grid=(t // tc,),
        in_specs=[pl.BlockSpec((b, tc, width), lambda i: (0, i, 0)),
                  const(w_units.shape), const(c_units.shape), const(tab.shape),
                  const((1, width)), const(w_glu_bf.shape), const((1, width))],
        out_specs=(pl.BlockSpec((b, tc, width), lambda i: (0, i, 0)),
                   const((b, 1, states)), const((b, 1, states))),
        out_shape=(jax.ShapeDtypeStruct((b, t, width), F32),
                   jax.ShapeDtypeStruct((b, 1, states), F32),
                   jax.ShapeDtypeStruct((b, 1, states), F32)),
        scratch_shapes=[pltpu.VMEM((b, n_units, tc, sw2), F32),
                        pltpu.VMEM((b, n_units, tc + SUBLANES, sw2), F32)],
        compiler_params=_cparams(1),
        name="ssm_prompt",
    )(u, w_units, c_units, tab, d_flat, w_glu_bf, b_glu)


def _ssm_s_kernel(u_ref, h0r_ref, h0i_ref, lam_ref, wre_ref, wim_ref, cre_ref, cim_ref,
                  d_ref, wg_ref, bg_ref, o_ref, hre_ref, him_ref):
    u = u_ref[...]
    lr, li = lam_ref[0:1, :], lam_ref[1:2, :]
    h0r, h0i = h0r_ref[...], h0i_ref[...]
    hr = (lr * h0r - li * h0i) + _dot(u, wre_ref[...], True)
    hi = (lr * h0i + li * h0r) + _dot(u, wim_ref[...], True)
    hre_ref[...] = hr
    him_ref[...] = hi
    y = _dot(hr, cre_ref[...], True) - _dot(hi, cim_ref[...], True) + d_ref[...] * u
    o_ref[...] = _gelu_glu(y, wg_ref, bg_ref, True)


def _ssm_sample(u, h0r, h0i, lam2, wre, wim, cre, cim, d_flat, w_glu, b_glu):
    rows, width = u.shape
    states = h0r.shape[1]
    args = (u, h0r, h0i, lam2, wre, wim, cre, cim, d_flat, w_glu, b_glu)
    full = lambda a: pl.BlockSpec(a.shape, lambda i: (0,) * a.ndim)
    return pl.pallas_call(
        _ssm_s_kernel,
        grid=(1,),
        in_specs=[full(a) for a in args],
        out_specs=(pl.BlockSpec((rows, width), lambda i: (0, 0)),
                   pl.BlockSpec((rows, states), lambda i: (0, 0)),
                   pl.BlockSpec((rows, states), lambda i: (0, 0))),
        out_shape=(jax.ShapeDtypeStruct((rows, width), F32),
                   jax.ShapeDtypeStruct((rows, states), F32),
                   jax.ShapeDtypeStruct((rows, states), F32)),
        compiler_params=_cparams(1),
        name="ssm_sample",
    )(*args)


def _post_kernel(x_ref, oa_ref, os_ref, g1_ref, sh2_ref, sc2_ref, g2_ref,
                 ga_ref, gs_ref, gpm_ref, gpf_ref, gff_ref, wo_ref, wu_ref, wd_ref, y_ref,
                 *, precise):
    merged = jnp.concatenate([_rms(oa_ref[...], ga_ref[...]), _rms(os_ref[...], gs_ref[...])],
                             axis=-1)
    x1 = x_ref[...] + g1_ref[...] * _rms(_dot(merged, wo_ref[...], precise), gpm_ref[...])
    h = _rms(x1, gpf_ref[...]) * (1.0 + sc2_ref[...]) + sh2_ref[...]
    up = jnp.maximum(_dot(h, wu_ref[...], precise), 0.0)
    f = _dot(up * up, wd_ref[...], precise)
    y_ref[...] = x1 + g2_ref[...] * _rms(f, gff_ref[...])


def _post_prompt(x, o_attn, o_ssm, mod, gains, w_out, w_up, w_down):
    b, t, d = x.shape
    aw = o_attn.shape[-1]
    tb = min(PROJ_ROWS, t)
    row = lambda bi, i: (bi, i, 0)
    modspec = lambda which: pl.BlockSpec((None, None, 1, d), lambda bi, i: (bi, which, 0, 0))
    const = lambda a: pl.BlockSpec(a.shape, lambda bi, i: (0,) * a.ndim,
                                   pipeline_mode=pl.Buffered(1))
    return pl.pallas_call(
        functools.partial(_post_kernel, precise=False),
        grid=(b, t // tb),
        in_specs=[pl.BlockSpec((None, tb, d), row),
                  pl.BlockSpec((None, tb, aw), row), pl.BlockSpec((None, tb, aw), row),
                  modspec(2), modspec(3), modspec(4), modspec(5)]
                 + [const(g) for g in gains] + [const(w_out), const(w_up), const(w_down)],
        out_specs=pl.BlockSpec((None, tb, d), row),
        out_shape=jax.ShapeDtypeStruct((b, t, d), F32),
        compiler_params=_cparams(2),
        name="post_prompt",
    )(x, o_attn, o_ssm, mod, mod, mod, mod, *gains, w_out, w_up, w_down)


def _post_sample(x, o_attn, o_ssm, g1, sh2, sc2, g2, gains, w_out, w_up, w_down):
    rows, d = x.shape
    ff = w_up.shape[1]
    args = (x, o_attn, o_ssm, g1, sh2, sc2, g2, *gains, w_out, w_up, w_down)
    full = lambda a: pl.BlockSpec(a.shape, lambda i: (0,) * a.ndim, pipeline_mode=pl.Buffered(1))
    return pl.pallas_call(
        functools.partial(_post_kernel, precise=False),
        grid=(1,),
        in_specs=[full(a) for a in args],
        out_specs=pl.BlockSpec((rows, d), lambda i: (0, 0)),
        out_shape=jax.ShapeDtypeStruct((rows, d), F32),
        compiler_params=_cparams(1),
        name="post_sample",
    )(*args)


def _block_diag(blocks):
    g, r, c = blocks.shape
    eye = jnp.eye(g, dtype=blocks.dtype)
    return (blocks[:, :, None, :] * eye[:, None, :, None]).reshape(g * r, g * c)


def _layer(xp, xs, c_prompt, c_sample, pool_k, pool_v, h0_re, h0_im, page_table, p):
    b, t, d = xp.shape
    db = xs.shape[0]
    g, pstate = p["ssm_lam_re"].shape
    n_heads = p["sb_bias"].shape[0]
    attn_w = n_heads * HEAD_DIM
    ssm_w = g * SSM_GROUP
    states = g * pstate

    bbre, bbim, tab = _prep(p["ssm_lam_re"], p["ssm_lam_im"], p["ssm_log_dt"],
                            p["ssm_b_re"], p["ssm_b_im"])
    wre = _block_diag(bbre)
    wim = _block_diag(bbim)
    cre = _block_diag(p["ssm_c_re"].transpose(0, 2, 1))
    cim = _block_diag(p["ssm_c_im"].transpose(0, 2, 1))
    n_units = g // SSM_UNIT
    uw, sw = SSM_UNIT * SSM_GROUP, SSM_UNIT * pstate
    w_units = jnp.stack([jnp.concatenate([wre[m * uw:(m + 1) * uw, m * sw:(m + 1) * sw],
                                          wim[m * uw:(m + 1) * uw, m * sw:(m + 1) * sw]], axis=1)
                         for m in range(n_units)]).astype(BF16)
    c_units = jnp.stack([jnp.concatenate([cre[m * sw:(m + 1) * sw, m * uw:(m + 1) * uw],
                                          -cim[m * sw:(m + 1) * sw, m * uw:(m + 1) * uw]], axis=0)
                         for m in range(n_units)]).astype(BF16)
    d_flat = p["ssm_d"].reshape(1, ssm_w)
    b_glu = p["b_glu"].reshape(1, ssm_w)
    bias2 = p["sb_bias"] * LOG2E
    q_scale = HEAD_DIM ** -0.5 * LOG2E
    row1 = lambda a: a.reshape(1, -1)
    gains = [row1(p[n]) for n in ("g_attn_out", "g_ssm_out", "g_post_mix", "g_pre_ffn",
                                  "g_post_ffn")]

    n_mod = b + db
    pad = (-n_mod) % SUBLANES
    c_all = jnp.concatenate([c_prompt, c_sample, jnp.zeros((pad, d), F32)], axis=0)
    mod = _ada(c_all, p["w_ada"], p["b_ada"])
    mod_p = mod[:b].reshape(b, 6, 1, d)
    mod_s = mod[b:b + db].reshape(db, 6, d)

    w_in_bf = p["w_in"].astype(BF16)
    q2, k, v, u, kb, vb = _proj_prompt(xp, mod_p, row1(p["g_pre_mix"]), w_in_bf, q_scale)
    ot = _attn_prompt(bias2, q2.transpose(0, 2, 1), kb, vb)
    o_attn = ot.transpose(0, 2, 1)
    o_ssm, hre_p, him_p = _ssm_prompt(u, w_units, c_units, tab, d_flat,
                                      p["w_glu"].astype(BF16), b_glu)
    w_post = [p[n].astype(BF16) for n in ("w_out", "w_up", "w_down")]
    yp = _post_prompt(xp, o_attn, o_ssm, mod_p, gains, *w_post)

    xs2 = xs.reshape(db, d)
    qs, ks, vs, us = _proj_sample(xs2, mod_s[:, 0], mod_s[:, 1], row1(p["g_pre_mix"]),
                                  p["w_in"], q_scale)
    by_head = lambda a: a.reshape(db, n_heads, HEAD_DIM)
    oa_s = _attn_sample(page_table, by_head(qs), by_head(ks), by_head(vs), bias2,
                        pool_k, pool_v).reshape(db, attn_w)
    lam2 = jnp.concatenate([tab[6, 0:1], tab[7, 0:1]], axis=0)
    os_s, hre_s, him_s = _ssm_sample(us, h0_re.reshape(db, states), h0_im.reshape(db, states),
                                     lam2, wre, wim, cre, cim, d_flat, p["w_glu"], b_glu)
    ys = _post_sample(xs2, oa_s, os_s, mod_s[:, 2], mod_s[:, 3], mod_s[:, 4], mod_s[:, 5],
                      gains, *w_post)

    heads = (n_heads, HEAD_DIM)
    return (yp, ys.reshape(db, 1, d),
            k.reshape(b, t, *heads), v.reshape(b, t, *heads),
            hre_p.reshape(b, g, pstate), him_p.reshape(b, g, pstate),
            ks.reshape(db, 1, *heads), vs.reshape(db, 1, *heads),
            hre_s.reshape(db, g, pstate), him_s.reshape(db, g, pstate))


def kernel(x_prompt, x_sample, c_prompt, c_sample, cache_k, cache_v, state_ssm_re, state_ssm_im, page_table, w_ada, b_ada, g_pre_mix, w_in, sb_bias, ssm_lam_re, ssm_lam_im, ssm_log_dt, ssm_b_re, ssm_b_im, ssm_c_re, ssm_c_im, ssm_d, w_glu, b_glu, g_attn_out, g_ssm_out, w_out, g_post_mix, g_pre_ffn, w_up, w_down, g_post_ffn):
    weights = dict(w_ada=w_ada, b_ada=b_ada, g_pre_mix=g_pre_mix, w_in=w_in, sb_bias=sb_bias,
                   ssm_lam_re=ssm_lam_re, ssm_lam_im=ssm_lam_im, ssm_log_dt=ssm_log_dt,
                   ssm_b_re=ssm_b_re, ssm_b_im=ssm_b_im, ssm_c_re=ssm_c_re, ssm_c_im=ssm_c_im,
                   ssm_d=ssm_d, w_glu=w_glu, b_glu=b_glu, g_attn_out=g_attn_out,
                   g_ssm_out=g_ssm_out, w_out=w_out, g_post_mix=g_post_mix,
                   g_pre_ffn=g_pre_ffn, w_up=w_up, w_down=w_down, g_post_ffn=g_post_ffn)
    depth = w_in.shape[0]
    assert x_sample.shape[1] == 1, "decode path handles one new token per sequence"
    xp, xs = x_prompt, x_sample
    outs = []
    for l in range(depth):
        p = {n: a[l] for n, a in weights.items()}
        res = _layer(xp, xs, c_prompt, c_sample, cache_k[l], cache_v[l],
                     state_ssm_re[l], state_ssm_im[l], page_table, p)
        xp, xs = res[0], res[1]
        outs.append(res[2:])
    stacked = tuple(jnp.stack([o[i] for o in outs]) for i in range(8))
    return (xp, xs) + stacked
```

```python
import functools
import math

import numpy as np
import jax
import jax.numpy as jnp
from jax import lax
from jax.experimental import pallas as pl
from jax.experimental.pallas import tpu as pltpu

F32 = jnp.float32
BF16 = jnp.bfloat16
HIGHEST = lax.Precision.HIGHEST

RMS_EPS = 1e-6
HEAD_DIM = 64
SSM_GROUP = 16
SSM_STATE = 64
LOG2E = 1.4426950408889634
LN2 = 0.6931471805599453

LANES = 128
SUBLANES = 8
KEY_BLOCK = 128
Q_WIDTH = 512
PROJ_ROWS = 512
SSM_ROWS = 256
PAGES_PER_STEP = 8
SSM_UNIT = 16
VMEM_LIMIT = 56 * 1024 * 1024


def _cparams(n_axes):
    return pltpu.CompilerParams(dimension_semantics=("arbitrary",) * n_axes,
                                vmem_limit_bytes=VMEM_LIMIT)


def _rms(x, g):
    inv = lax.rsqrt(jnp.mean(x * x, axis=-1, keepdims=True) + RMS_EPS)
    return (x * inv) * g


def _dot(a, b, precise):
    if precise:
        return jnp.dot(a, b, precision=HIGHEST, preferred_element_type=F32)
    return jnp.dot(a.astype(BF16), b.astype(BF16), preferred_element_type=F32)


def _softplus2(z2):
    neg_abs = lax.bitcast_convert_type(
        lax.bitcast_convert_type(z2, jnp.uint32) | jnp.uint32(0x80000000), F32)
    e = jnp.exp2(neg_abs)
    return jnp.maximum(z2, 0.0) + jnp.log(1.0 + e) * LOG2E


def _split_bf16(p):
    hi = lax.bitcast_convert_type(
        lax.bitcast_convert_type(p, jnp.uint32) & jnp.uint32(0xFFFF0000), F32)
    return hi.astype(BF16), (p - hi).astype(BF16)


def _prep_kernel(lre_ref, lim_ref, dt_ref, bre_ref, bim_ref, lre8_ref, lim8_ref, dt8_ref,
                 bbre_ref, bbim_ref, tab_ref):
    def lam_bar(lre, lim, log_dt):
        dt = jnp.exp(log_dt)
        mag = jnp.exp(lre * dt)
        ang = lim * dt
        return mag * jnp.cos(ang), mag * jnp.sin(ang)

    lre, lim = lre_ref[...], lim_ref[...]
    lbr, lbi = lam_bar(lre, lim, dt_ref[...])
    nr, ni = lbr - 1.0, lbi
    den = lre * lre + lim * lim
    cre = (nr * lre + ni * lim) / den
    cim = (ni * lre - nr * lim) / den
    bre, bim = bre_ref[...], bim_ref[...]
    bbre_ref[...] = cre * bre - cim * bim
    bbim_ref[...] = cre * bim + cim * bre

    p1r, p1i = lam_bar(lre8_ref[...], lim8_ref[...], dt8_ref[...])
    row = lax.broadcasted_iota(jnp.int32, p1r.shape, 0)

    def cmul(ar, ai, br, bi):
        return ar * br - ai * bi, ar * bi + ai * br

    p2r, p2i = cmul(p1r, p1i, p1r, p1i)
    p4r, p4i = cmul(p2r, p2i, p2r, p2i)
    p8r, p8i = cmul(p4r, p4i, p4r, p4i)
    e = row + 1
    accr, acci = jnp.ones_like(p1r), jnp.zeros_like(p1r)
    for bit, (pr, pi) in enumerate(((p1r, p1i), (p2r, p2i), (p4r, p4i), (p8r, p8i))):
        on = ((e >> bit) & 1) == 1
        nr_, ni_ = cmul(accr, acci, pr, pi)
        accr = jnp.where(on, nr_, accr)
        acci = jnp.where(on, ni_, acci)
    zero = jnp.zeros_like(p1r)
    tab_ref[0] = jnp.where(row >= 1, p1r, zero)
    tab_ref[1] = jnp.where(row >= 1, p1i, zero)
    tab_ref[2] = jnp.where(row >= 2, p2r, zero)
    tab_ref[3] = jnp.where(row >= 2, p2i, zero)
    tab_ref[4] = jnp.where(row >= 4, p4r, zero)
    tab_ref[5] = jnp.where(row >= 4, p4i, zero)
    tab_ref[6] = accr
    tab_ref[7] = acci


def _prep(lam_re, lam_im, log_dt, b_re, b_im):
    g, p = lam_re.shape
    c = b_re.shape[-1]
    rows = g * c

    def rep(a):
        return jnp.broadcast_to(a[:, None, :], (g, c, p)).reshape(rows, p)

    def flat8(a):
        return jnp.broadcast_to(a.reshape(1, g * p), (SUBLANES, g * p))

    dt_gp = jnp.broadcast_to(log_dt[:, None], (g, p))
    bt = lambda b: b.transpose(0, 2, 1).reshape(rows, p)
    bbre, bbim, tab = pl.pallas_call(
        _prep_kernel,
        out_shape=(jax.ShapeDtypeStruct((rows, p), F32),
                   jax.ShapeDtypeStruct((rows, p), F32),
                   jax.ShapeDtypeStruct((8, SUBLANES, g * p), F32)),
        name="prep",
    )(rep(lam_re), rep(lam_im), rep(dt_gp), bt(b_re), bt(b_im),
      flat8(lam_re), flat8(lam_im), flat8(dt_gp))
    return bbre.reshape(g, c, p), bbim.reshape(g, c, p), tab


def _ada_kernel(c_ref, w_ref, b_ref, o_ref):
    c = c_ref[...]
    s = c * jax.nn.sigmoid(c)
    o_ref[...] = _dot(s, w_ref[...], True) + b_ref[...]


def _ada(c_all, w_ada, b_ada):
    rows, d = c_all.shape
    n = w_ada.shape[1]
    bn = 1024
    return pl.pallas_call(
        _ada_kernel,
        grid=(n // bn,),
        in_specs=[pl.BlockSpec((rows, d), lambda j: (0, 0)),
                  pl.BlockSpec((d, bn), lambda j: (0, j)),
                  pl.BlockSpec((1, bn), lambda j: (0, j))],
        out_specs=pl.BlockSpec((rows, bn), lambda j: (0, j)),
        out_shape=jax.ShapeDtypeStruct((rows, n), F32),
        compiler_params=_cparams(1),
        name="ada",
    )(c_all, w_ada, b_ada.reshape(1, n))


def _proj_kernel(x_ref, sh_ref, sc_ref, g_ref, w_ref, *out_refs, attn_w, precise, q_scale):
    h = _rms(x_ref[...], g_ref[...]) * (1.0 + sc_ref[...]) + sh_ref[...]
    p = _dot(h, w_ref[...], precise)
    q, k, v, u = (p[:, i * attn_w:(i + 1) * attn_w] for i in range(4))
    if precise:
        q_ref, k_ref, v_ref, u_ref = out_refs
        q_ref[...] = q * q_scale
    else:
        q_ref, k_ref, v_ref, u_ref, kb_ref, vb_ref = out_refs
        q_ref[...] = (q * q_scale).astype(BF16)
        kb_ref[...] = k.astype(BF16)
        vb_ref[...] = v.astype(BF16)
    k_ref[...] = k
    v_ref[...] = v
    u_ref[...] = u


def _proj_prompt(x, mod, g_pre, w_in_bf, q_scale):
    b, t, d = x.shape
    n = w_in_bf.shape[1]
    aw = n // 4
    tb = min(PROJ_ROWS, t)
    row = lambda bi, i: (bi, i, 0)
    modspec = lambda which: pl.BlockSpec((None, None, 1, d), lambda bi, i: (bi, which, 0, 0))
    f32o = jax.ShapeDtypeStruct((b, t, aw), F32)
    bfo = jax.ShapeDtypeStruct((b, t, aw), BF16)
    ospec = pl.BlockSpec((None, tb, aw), row)
    return pl.pallas_call(
        functools.partial(_proj_kernel, attn_w=aw, precise=False, q_scale=q_scale),
        grid=(b, t // tb),
        in_specs=[pl.BlockSpec((None, tb, d), row), modspec(0), modspec(1),
                  pl.BlockSpec((1, d), lambda bi, i: (0, 0)),
                  pl.BlockSpec((d, n), lambda bi, i: (0, 0))],
        out_specs=(ospec,) * 6,
        out_shape=(bfo, f32o, f32o, f32o, bfo, bfo),
        compiler_params=_cparams(2),
        name="proj_prompt",
    )(x, mod, mod, g_pre, w_in_bf)


def _proj_sample(x, sh, sc, g_pre, w_in, q_scale):
    rows, d = x.shape
    n = w_in.shape[1]
    aw = n // 4
    full = lambda shape: pl.BlockSpec(shape, lambda i: (0,) * len(shape))
    o = jax.ShapeDtypeStruct((rows, aw), F32)
    return pl.pallas_call(
        functools.partial(_proj_kernel, attn_w=aw, precise=True, q_scale=q_scale),
        grid=(1,),
        in_specs=[full((rows, d)), full((rows, d)), full((rows, d)), full((1, d)), full((d, n))],
        out_specs=(full((rows, aw)),) * 4,
        out_shape=(o, o, o, o),
        compiler_params=_cparams(1),
        name="proj_sample",
    )(x, sh, sc, g_pre, w_in)


BF16_TILE_ROWS = 16
NU_DIM = KEY_BLOCK + BF16_TILE_ROWS
CARRY_ROWS = 3
MASKED = -1e30
PIPE = 4


def _suffix_matrix():
    s = np.arange(NU_DIM)[:, None]
    j = np.arange(NU_DIM)[None, :]
    key = j < KEY_BLOCK
    m = np.where(key & ((j >= s) | (s >= KEY_BLOCK)), -1.0, 0.0)
    m = m + np.where((s < KEY_BLOCK) & (j >= KEY_BLOCK) & (j < KEY_BLOCK + CARRY_ROWS), 1.0, 0.0)
    return jnp.asarray(m, dtype=BF16)


def _attn_p_kernel(bias_ref, qt_ref, k_ref, vt_ref, nu_ref, o_ref,
                   qm_s, bm_s, z_s, p_s, tot_s, a_s, acc_s, carry_s):
    hp = pl.program_id(1)
    qi = pl.program_id(2)
    qw = qt_ref.shape[1]
    n_chunks = qw // LANES
    bpq = qw // KEY_BLOCK
    zeros = lambda ref, *idx: jnp.zeros(ref.shape[len(idx):], ref.dtype)

    @pl.when(qi == 0)
    def _():
        s_io = lax.broadcasted_iota(jnp.int32, (KEY_BLOCK, qw), 0)
        t_io = lax.broadcasted_iota(jnp.int32, (KEY_BLOCK, qw), 1)
        for hh in range(2):
            bias2 = bias_ref[2 * hp + hh]
            bm_s[hh, 0] = jnp.full((KEY_BLOCK, qw), bias2, F32)
            for rel in range(bpq):
                vis = s_io + rel * KEY_BLOCK < t_io
                bm_s[hh, rel + 1] = jnp.where(vis, bias2, MASKED)
            bm_s[hh, bpq + 1] = jnp.full((KEY_BLOCK, qw), MASKED, F32)

    q32 = qt_ref[...].astype(F32)
    row = lax.broadcasted_iota(jnp.int32, q32.shape, 0)
    for hh in range(2):
        own = (row >= hh * HEAD_DIM) & (row < (hh + 1) * HEAD_DIM)
        qm_s[hh] = jnp.where(own, q32, 0.0).astype(BF16)
        acc_s[hh] = zeros(acc_s, hh)
        carry_s[hh] = zeros(carry_s, hh)
        z_s[hh, 2] = jnp.full(z_s.shape[2:], MASKED, F32)
        z_s[hh, 3] = jnp.full(z_s.shape[2:], MASKED, F32)
        p_s[hh, 0] = zeros(p_s, hh, 0)
        tot_s[hh, 1] = zeros(tot_s, hh, 1)
        a_s[hh, 1] = zeros(a_s, hh, 1)

    last = (qi + 1) * bpq - 1
    row8 = lax.broadcasted_iota(jnp.int32, (SUBLANES, qw), 0)
    top16 = lambda x: lax.bitcast_convert_type(
        lax.bitcast_convert_type(x, jnp.uint32) & jnp.uint32(0xFFFF0000), F32)

    def stage(i, u):
        e, o = u % 2, (u + 1) % 2
        j_in = jnp.maximum(last - i, 0)
        k_in = k_ref[pl.ds(pl.multiple_of(j_in * KEY_BLOCK, KEY_BLOCK), KEY_BLOCK), :]
        m_in = jnp.where(i > last, bpq + 1, jnp.clip(j_in - qi * bpq, -1, bpq - 1) + 1)
        j_out = jnp.clip(last - (i - 3), 0, last)
        for hh in range(2):
            z_s[hh, u] = jnp.dot(k_in, qm_s[hh], preferred_element_type=F32) + bm_s[hh, m_in]
        for hh in range(2):
            vt = vt_ref[j_out, hh * HEAD_DIM:(hh + 1) * HEAD_DIM, :]
            acc_s[hh] += jnp.dot(vt, a_s[hh, o], preferred_element_type=F32)
        r = []
        for hh in range(2):
            carry = carry_s[hh] + tot_s[hh, o]
            carry_s[hh] = carry
            c_hi = top16(carry)
            c_mid = top16(carry - c_hi)
            c_lo = (carry - c_hi) - c_mid
            addends = jnp.where(row8 == 0, c_hi, jnp.where(row8 == 1, c_mid,
                                                           jnp.where(row8 == 2, c_lo, 0.0)))
            p_s[hh, e, KEY_BLOCK:NU_DIM, :] = jnp.concatenate(
                [addends, jnp.zeros_like(addends)], axis=0).astype(BF16)
            r.append(jnp.dot(nu_ref[...], p_s[hh, e], preferred_element_type=F32))
            tot_s[hh, e] = r[hh][KEY_BLOCK:KEY_BLOCK + SUBLANES, :]
        for hh in range(2):
            for c in range(n_chunks):
                sl = slice(c * LANES, (c + 1) * LANES)
                p_s[hh, o, 0:KEY_BLOCK, sl] = _softplus2(z_s[hh, (u - 1) % PIPE, :, sl]).astype(BF16)
        for hh in range(2):
            for c in range(n_chunks):
                sl = slice(c * LANES, (c + 1) * LANES)
                log_a = z_s[hh, (u - 2) % PIPE, :, sl] + r[hh][0:KEY_BLOCK, sl]
                a_s[hh, e, :, sl] = jnp.exp2(log_a).astype(BF16)

    def body(n, c):
        for u in range(PIPE):
            stage(PIPE * n + u, u)
        return c

    lax.fori_loop(0, ((qi + 1) * bpq + 3 + PIPE - 1) // PIPE, body, 0)

    for hh in range(2):
        o_ref[hh * HEAD_DIM:(hh + 1) * HEAD_DIM, :] = acc_s[hh]


def _attn_prompt(bias2, qt, kb, vb):
    b, w, t = qt.shape
    qw = Q_WIDTH
    bpq = qw // KEY_BLOCK
    assert t % qw == 0 and bpq % 2 == 0
    pair = 2 * HEAD_DIM
    nkb = t // KEY_BLOCK
    vt = vb.reshape(b, nkb, KEY_BLOCK, w).transpose(0, 1, 3, 2)
    return pl.pallas_call(
        _attn_p_kernel,
        grid=(b, w // pair, t // qw),
        in_specs=[pl.BlockSpec(memory_space=pltpu.SMEM),
                  pl.BlockSpec((None, pair, qw), lambda bi, hp, qi: (bi, hp, qi)),
                  pl.BlockSpec((None, t, pair), lambda bi, hp, qi: (bi, 0, hp)),
                  pl.BlockSpec((None, nkb, pair, KEY_BLOCK), lambda bi, hp, qi: (bi, 0, hp, 0)),
                  pl.BlockSpec((NU_DIM, NU_DIM), lambda bi, hp, qi: (0, 0))],
        out_specs=pl.BlockSpec((None, pair, qw), lambda bi, hp, qi: (bi, hp, qi)),
        out_shape=jax.ShapeDtypeStruct((b, w, t), F32),
        scratch_shapes=[pltpu.VMEM((2, pair, qw), BF16),
                        pltpu.VMEM((2, bpq + 2, KEY_BLOCK, qw), F32),
                        pltpu.VMEM((2, PIPE, KEY_BLOCK, qw), F32),
                        pltpu.VMEM((2, 2, NU_DIM, qw), BF16),
                        pltpu.VMEM((2, 2, SUBLANES, qw), F32),
                        pltpu.VMEM((2, 2, KEY_BLOCK, qw), BF16),
                        pltpu.VMEM((2, HEAD_DIM, qw), F32),
                        pltpu.VMEM((2, SUBLANES, qw), F32)],
        compiler_params=_cparams(3),
        name="attn_prompt",
    )(bias2, qt, kb, vt, _suffix_matrix())


def _neg_suffix_matrix_lanes():
    j = np.arange(2 * KEY_BLOCK)[:, None] % KEY_BLOCK
    s = np.arange(2 * KEY_BLOCK)[None, :]
    return jnp.asarray(np.where((j >= s) | (s >= KEY_BLOCK), -1.0, 0.0), dtype=BF16)


def _attn_s_kernel(pt_ref, q_ref, kn_ref, vn_ref, bias_ref, nu_ref, *refs):
    pages = PAGES_PER_STEP
    k_refs, v_refs = refs[:pages], refs[pages:2 * pages]
    o_ref = refs[2 * pages]
    qx_s, acc_s, carry_s = refs[2 * pages + 1:]
    g = pl.program_id(1)
    rows, w = qx_s.shape
    lane_head = lax.broadcasted_iota(jnp.int32, (rows, w), 1) // HEAD_DIM
    own = lane_head == lax.broadcasted_iota(jnp.int32, (rows, w), 0)

    @pl.when(g == 0)
    def _():
        qx = jnp.where(own, jnp.broadcast_to(q_ref[0], (rows, w)), 0.0)
        qx_s[...] = qx.astype(BF16)
        tpos = lax.broadcasted_iota(jnp.int32, (rows, 1), 1)
        vis = tpos < tpos
        z_new = jnp.sum(qx * kn_ref[0], axis=-1, keepdims=True) + bias_ref[...]
        sp = _softplus2(z_new)
        log_1m = jnp.where(vis, -sp, 0.0)
        a_new = jnp.where(vis, jnp.exp2(z_new - sp), 0.0)
        acc_s[...] = a_new * jnp.where(own, jnp.broadcast_to(vn_ref[0], (rows, w)), 0.0)
        carry_s[...] = jnp.broadcast_to(log_1m, carry_s.shape)

    lanes_of = lambda x, r: x[:, r * KEY_BLOCK:(r + 1) * KEY_BLOCK]
    side_by_side = lambda page_refs: jnp.concatenate(
        [ref[0].astype(BF16) for ref in page_refs], axis=1)
    z = jnp.dot(qx_s[...], side_by_side(k_refs),
                preferred_element_type=F32) + bias_ref[...]
    hi, lo = _split_bf16(jnp.concatenate([_softplus2(lanes_of(z, r)) for r in range(pages)],
                                         axis=0))
    rr = jnp.dot(jnp.concatenate([hi, lo], axis=1), nu_ref[...],
                 preferred_element_type=F32)
    carry = carry_s[...]
    weights = []
    for r in range(pages):
        blk = rr[r * rows:(r + 1) * rows]
        weights.append(jnp.exp2(lanes_of(z, r) + blk[:, :KEY_BLOCK] + carry))
        carry = carry + blk[:, KEY_BLOCK:]
    carry_s[...] = carry
    a = jnp.concatenate(weights, axis=1).astype(BF16)
    acc_s[...] += lax.dot_general(a, side_by_side(v_refs), (((1,), (1,)), ((), ())),
                                  preferred_element_type=F32)

    @pl.when(g == pl.num_programs(1) - 1)
    def _():
        o_ref[0] = jnp.sum(jnp.where(own, acc_s[...], 0.0), axis=0, keepdims=True)


def _attn_sample(page_table, q2, k_new, v_new, bias2, pool_kt, pool_vt):
    db, w = q2.shape
    n_pool, _, page_rows = pool_kt.shape
    n_heads = w // HEAD_DIM
    n_pages = page_table.shape[1]
    pages = PAGES_PER_STEP
    assert n_pages % pages == 0 and page_rows == KEY_BLOCK and n_heads <= SUBLANES
    rows = BF16_TILE_ROWS
    bias_col = jnp.zeros((rows, 1), F32).at[:n_heads, 0].set(bias2)
    tok = lambda a: a.reshape(db, 1, w)
    tokspec = pl.BlockSpec((1, 1, w), lambda s, g, pt: (s, 0, 0))

    def page_spec(r):
        return pl.BlockSpec((1, w, page_rows),
                            lambda s, g, pt: (pt[s, n_pages - 1 - (g * pages + r)], 0, 0))

    grid_spec = pltpu.PrefetchScalarGridSpec(
        num_scalar_prefetch=1,
        grid=(db, n_pages // pages),
        in_specs=[tokspec, tokspec, tokspec,
                  pl.BlockSpec((rows, 1), lambda s, g, pt: (0, 0)),
                  pl.BlockSpec((2 * KEY_BLOCK, 2 * KEY_BLOCK), lambda s, g, pt: (0, 0))]
                 + [page_spec(r) for r in range(pages)] * 2,
        out_specs=tokspec,
        scratch_shapes=[pltpu.VMEM((rows, w), BF16),
                        pltpu.VMEM((rows, w), F32),
                        pltpu.VMEM((rows, KEY_BLOCK), F32)])
    out = pl.pallas_call(
        _attn_s_kernel,
        grid_spec=grid_spec,
        out_shape=jax.ShapeDtypeStruct((db, 1, w), F32),
        compiler_params=_cparams(2),
        name="attn_sample",
    )(page_table, tok(q2), tok(k_new), tok(v_new), bias_col, _neg_suffix_matrix_lanes(),
      *([pool_kt] * pages), *([pool_vt] * pages))
    return out.reshape(db, w)


def _gelu_glu(y, wg_ref, bg_ref, precise):
    g = 0.5 * y * (1.0 + jnp.tanh(math.sqrt(2.0 / math.pi) * (y + 0.044715 * (y * y * y))))
    return g * jax.nn.sigmoid(_dot(g, wg_ref[...], precise) + bg_ref[...])


def _ssm_p_kernel(u_ref, w_ref, cm_ref, tab_ref, d_ref, wg_ref, bg_ref,
                  o_ref, hre_ref, him_ref, bu_s, h_s):
    step = pl.program_id(0)
    nb, tc, width = u_ref.shape
    n_units = w_ref.shape[0]
    uw = width // n_units
    sw = w_ref.shape[2] // 2
    n_tiles = tc // SUBLANES

    @pl.when(step == 0)
    def _():
        h_s[:, :, 0:SUBLANES, :] = jnp.zeros((nb, n_units, SUBLANES, 2 * sw), F32)

    @pl.when(step > 0)
    def _():
        h_s[:, :, 0:SUBLANES, :] = h_s[:, :, tc:tc + SUBLANES, :]

    for b in range(nb):
        for m in range(n_units):
            bu_s[b, m] = _dot(u_ref[b, :, m * uw:(m + 1) * uw], w_ref[m], False)

    def tile_body(tb, carry):
        r0 = pl.multiple_of(tb * SUBLANES, SUBLANES)
        for b in range(nb):
            for m in range(n_units):
                for c in range(sw // LANES):
                    re = slice(c * LANES, (c + 1) * LANES)
                    im = slice(sw + c * LANES, sw + (c + 1) * LANES)
                    tl = slice(m * sw + c * LANES, m * sw + (c + 1) * LANES)
                    xr = bu_s[b, m, pl.ds(r0, SUBLANES), re]
                    xi = bu_s[b, m, pl.ds(r0, SUBLANES), im]
                    for k, ti in ((1, 0), (2, 2), (4, 4)):
                        ar, ai = tab_ref[ti, :, tl], tab_ref[ti + 1, :, tl]
                        sr = pltpu.roll(xr, k, 0)
                        si = pltpu.roll(xi, k, 0)
                        xr, xi = xr + (ar * sr - ai * si), xi + (ar * si + ai * sr)
                    prev_r = h_s[b, m, pl.ds(r0, SUBLANES), re]
                    prev_i = h_s[b, m, pl.ds(r0, SUBLANES), im]
                    pr = jnp.broadcast_to(prev_r[SUBLANES - 1:SUBLANES, :], (SUBLANES, LANES))
                    pi = jnp.broadcast_to(prev_i[SUBLANES - 1:SUBLANES, :], (SUBLANES, LANES))
                    lr, li = tab_ref[6, :, tl], tab_ref[7, :, tl]
                    xr, xi = xr + (lr * pr - li * pi), xi + (lr * pi + li * pr)
                    h_s[b, m, pl.ds(r0 + SUBLANES, SUBLANES), re] = xr
                    h_s[b, m, pl.ds(r0 + SUBLANES, SUBLANES), im] = xi
        return carry

    lax.fori_loop(0, n_tiles, tile_body, 0)

    for b in range(nb):
        ys = [_dot(h_s[b, m, SUBLANES:SUBLANES + tc, :], cm_ref[m], False) for m in range(n_units)]
        y = jnp.concatenate(ys, axis=1) + d_ref[...] * u_ref[b]
        o_ref[b] = _gelu_glu(y, wg_ref, bg_ref, False)
        for m in range(n_units):
            last = h_s[b, m, tc + SUBLANES - 1:tc + SUBLANES, :]
            hre_ref[b, :, m * sw:(m + 1) * sw] = last[:, :sw]
            him_ref[b, :, m * sw:(m + 1) * sw] = last[:, sw:]


def _ssm_prompt(u, w_units, c_units, tab, d_flat, w_glu_bf, b_glu):
    b, t, width = u.shape
    tc = min(SSM_ROWS, t)
    n_units, uw, sw2 = w_units.shape
    states = tab.shape[-1]
    const = lambda shape: pl.BlockSpec(shape, lambda i: (0,) * len(shape))
    return pl.pallas_call(
        _ssm_p_kernel,
        grid=(t // tc,),
        in_specs=[pl.BlockSpec((b, tc, width), lambda i: (0, i, 0)),
                  const(w_units.shape), const(c_units.shape), const(tab.shape),
                  const((1, width)), const(w_glu_bf.shape), const((1, width))],
        out_specs=(pl.BlockSpec((b, tc, width), lambda i: (0, i, 0)),
                   const((b, 1, states)), const((b, 1, states))),
        out_shape=(jax.ShapeDtypeStruct((b, t, width), F32),
                   jax.ShapeDtypeStruct((b, 1, states), F32),
                   jax.ShapeDtypeStruct((b, 1, states), F32)),
        scratch_shapes=[pltpu.VMEM((b, n_units, tc, sw2), F32),
                        pltpu.VMEM((b, n_units, tc + SUBLANES, sw2), F32)],
        compiler_params=_cparams(1),
        name="ssm_prompt",
    )(u, w_units, c_units, tab, d_flat, w_glu_bf, b_glu)


def _ssm_s_kernel(u_ref, h0r_ref, h0i_ref, lam_ref, wre_ref, wim_ref, cre_ref, cim_ref,
                  d_ref, wg_ref, bg_ref, o_ref, hre_ref, him_ref):
    u = u_ref[...]
    lr, li = lam_ref[0:1, :], lam_ref[1:2, :]
    h0r, h0i = h0r_ref[...], h0i_ref[...]
    hr = (lr * h0r - li * h0i) + _dot(u, wre_ref[...], True)
    hi = (lr * h0i + li * h0r) + _dot(u, wim_ref[...], True)
    hre_ref[...] = hr
    him_ref[...] = hi
    y = _dot(hr, cre_ref[...], True) - _dot(hi, cim_ref[...], True) + d_ref[...] * u
    o_ref[...] = _gelu_glu(y, wg_ref, bg_ref, True)


def _ssm_sample(u, h0r, h0i, lam2, wre, wim, cre, cim, d_flat, w_glu, b_glu):
    rows, width = u.shape
    states = h0r.shape[1]
    args = (u, h0r, h0i, lam2, wre, wim, cre, cim, d_flat, w_glu, b_glu)
    full = lambda a: pl.BlockSpec(a.shape, lambda i: (0,) * a.ndim)
    return pl.pallas_call(
        _ssm_s_kernel,
        grid=(1,),
        in_specs=[full(a) for a in args],
        out_specs=(pl.BlockSpec((rows, width), lambda i: (0, 0)),
                   pl.BlockSpec((rows, states), lambda i: (0, 0)),
                   pl.BlockSpec((rows, states), lambda i: (0, 0))),
        out_shape=(jax.ShapeDtypeStruct((rows, width), F32),
                   jax.ShapeDtypeStruct((rows, states), F32),
                   jax.ShapeDtypeStruct((rows, states), F32)),
        compiler_params=_cparams(1),
        name="ssm_sample",
    )(*args)


def _post_kernel(x_ref, oa_ref, os_ref, g1_ref, sh2_ref, sc2_ref, g2_ref,
                 ga_ref, gs_ref, gpm_ref, gpf_ref, gff_ref, wo_ref, wu_ref, wd_ref, y_ref,
                 *, precise):
    merged = jnp.concatenate([_rms(oa_ref[...], ga_ref[...]), _rms(os_ref[...], gs_ref[...])],
                             axis=-1)
    x1 = x_ref[...] + g1_ref[...] * _rms(_dot(merged, wo_ref[...], precise), gpm_ref[...])
    h = _rms(x1, gpf_ref[...]) * (1.0 + sc2_ref[...]) + sh2_ref[...]
    up = jnp.maximum(_dot(h, wu_ref[...], precise), 0.0)
    f = _dot(up * up, wd_ref[...], precise)
    y_ref[...] = x1 + g2_ref[...] * _rms(f, gff_ref[...])


def _post_prompt(x, o_attn, o_ssm, mod, gains, w_out, w_up, w_down):
    b, t, d = x.shape
    aw = o_attn.shape[-1]
    tb = min(PROJ_ROWS, t)
    row = lambda bi, i: (bi, i, 0)
    modspec = lambda which: pl.BlockSpec((None, None, 1, d), lambda bi, i: (bi, which, 0, 0))
    const = lambda a: pl.BlockSpec(a.shape, lambda bi, i: (0,) * a.ndim,
                                   pipeline_mode=pl.Buffered(1))
    return pl.pallas_call(
        functools.partial(_post_kernel, precise=False),
        grid=(b, t // tb),
        in_specs=[pl.BlockSpec((None, tb, d), row),
                  pl.BlockSpec((None, tb, aw), row), pl.BlockSpec((None, tb, aw), row),
                  modspec(2), modspec(3), modspec(4), modspec(5)]
                 + [const(g) for g in gains] + [const(w_out), const(w_up), const(w_down)],
        out_specs=pl.BlockSpec((None, tb, d), row),
        out_shape=jax.ShapeDtypeStruct((b, t, d), F32),
        compiler_params=_cparams(2),
        name="post_prompt",
    )(x, o_attn, o_ssm, mod, mod, mod, mod, *gains, w_out, w_up, w_down)


def _post_sample(x, o_attn, o_ssm, g1, sh2, sc2, g2, gains, w_out, w_up, w_down):
    rows, d = x.shape
    ff = w_up.shape[1]
    args = (x, o_attn, o_ssm, g1, sh2, sc2, g2, *gains, w_out, w_up, w_down)
    full = lambda a: pl.BlockSpec(a.shape, lambda i: (0,) * a.ndim, pipeline_mode=pl.Buffered(1))
    return pl.pallas_call(
        functools.partial(_post_kernel, precise=False),
        grid=(1,),
        in_specs=[full(a) for a in args],
        out_specs=pl.BlockSpec((rows, d), lambda i: (0, 0)),
        out_shape=jax.ShapeDtypeStruct((rows, d), F32),
        compiler_params=_cparams(1),
        name="post_sample",
    )(*args)


def _block_diag(blocks):
    g, r, c = blocks.shape
    eye = jnp.eye(g, dtype=blocks.dtype)
    return (blocks[:, :, None, :] * eye[:, None, :, None]).reshape(g * r, g * c)


def _layer(xp, xs, c_prompt, c_sample, pool_k, pool_v, h0_re, h0_im, page_table, p):
    b, t, d = xp.shape
    db = xs.shape[0]
    g, pstate = p["ssm_lam_re"].shape
    n_heads = p["sb_bias"].shape[0]
    attn_w = n_heads * HEAD_DIM
    ssm_w = g * SSM_GROUP
    states = g * pstate

    bbre, bbim, tab = _prep(p["ssm_lam_re"], p["ssm_lam_im"], p["ssm_log_dt"],
                            p["ssm_b_re"], p["ssm_b_im"])
    wre = _block_diag(bbre)
    wim = _block_diag(bbim)
    cre = _block_diag(p["ssm_c_re"].transpose(0, 2, 1))
    cim = _block_diag(p["ssm_c_im"].transpose(0, 2, 1))
    n_units = g // SSM_UNIT
    uw, sw = SSM_UNIT * SSM_GROUP, SSM_UNIT * pstate
    w_units = jnp.stack([jnp.concatenate([wre[m * uw:(m + 1) * uw, m * sw:(m + 1) * sw],
                                          wim[m * uw:(m + 1) * uw, m * sw:(m + 1) * sw]], axis=1)
                         for m in range(n_units)]).astype(BF16)
    c_units = jnp.stack([jnp.concatenate([cre[m * sw:(m + 1) * sw, m * uw:(m + 1) * uw],
                                          -cim[m * sw:(m + 1) * sw, m * uw:(m + 1) * uw]], axis=0)
                         for m in range(n_units)]).astype(BF16)
    d_flat = p["ssm_d"].reshape(1, ssm_w)
    b_glu = p["b_glu"].reshape(1, ssm_w)
    bias2 = p["sb_bias"] * LOG2E
    q_scale = HEAD_DIM ** -0.5 * LOG2E
    row1 = lambda a: a.reshape(1, -1)
    gains = [row1(p[n]) for n in ("g_attn_out", "g_ssm_out", "g_post_mix", "g_pre_ffn",
                                  "g_post_ffn")]

    n_mod = b + db
    pad = (-n_mod) % SUBLANES
    c_all = jnp.concatenate([c_prompt, c_sample, jnp.zeros((pad, d), F32)], axis=0)
    mod = _ada(c_all, p["w_ada"], p["b_ada"])
    mod_p = mod[:b].reshape(b, 6, 1, d)
    mod_s = mod[b:b + db].reshape(db, 6, d)

    w_in_bf = p["w_in"].astype(BF16)
    q2, k, v, u, kb, vb = _proj_prompt(xp, mod_p, row1(p["g_pre_mix"]), w_in_bf, q_scale)
    ot = _attn_prompt(bias2, q2.transpose(0, 2, 1), kb, vb)
    o_attn = ot.transpose(0, 2, 1)
    o_ssm, hre_p, him_p = _ssm_prompt(u, w_units, c_units, tab, d_flat,
                                      p["w_glu"].astype(BF16), b_glu)
    w_post = [p[n].astype(BF16) for n in ("w_out", "w_up", "w_down")]
    yp = _post_prompt(xp, o_attn, o_ssm, mod_p, gains, *w_post)

    xs2 = xs.reshape(db, d)
    qs, ks, vs, us = _proj_sample(xs2, mod_s[:, 0], mod_s[:, 1], row1(p["g_pre_mix"]),
                                  p["w_in"], q_scale)
    n_pool, page = pool_k.shape[:2]
    by_pos = lambda pool: pool.transpose(0, 2, 3, 1).reshape(n_pool, attn_w, page)
    oa_s = _attn_sample(page_table, qs, ks, vs, bias2, by_pos(pool_k), by_pos(pool_v))
    lam2 = jnp.concatenate([tab[6, 0:1], tab[7, 0:1]], axis=0)
    os_s, hre_s, him_s = _ssm_sample(us, h0_re.reshape(db, states), h0_im.reshape(db, states),
                                     lam2, wre, wim, cre, cim, d_flat, p["w_glu"], b_glu)
    ys = _post_sample(xs2, oa_s, os_s, mod_s[:, 2], mod_s[:, 3], mod_s[:, 4], mod_s[:, 5],
                      gains, *w_post)

    heads = (n_heads, HEAD_DIM)
    return (yp, ys.reshape(db, 1, d),
            k.reshape(b, t, *heads), v.reshape(b, t, *heads),
            hre_p.reshape(b, g, pstate), him_p.reshape(b, g, pstate),
            ks.reshape(db, 1, *heads), vs.reshape(db, 1, *heads),
            hre_s.reshape(db, g, pstate), him_s.reshape(db, g, pstate))


def kernel(x_prompt, x_sample, c_prompt, c_sample, cache_k, cache_v, state_ssm_re, state_ssm_im, page_table, w_ada, b_ada, g_pre_mix, w_in, sb_bias, ssm_lam_re, ssm_lam_im, ssm_log_dt, ssm_b_re, ssm_b_im, ssm_c_re, ssm_c_im, ssm_d, w_glu, b_glu, g_attn_out, g_ssm_out, w_out, g_post_mix, g_pre_ffn, w_up, w_down, g_post_ffn):
    weights = dict(w_ada=w_ada, b_ada=b_ada, g_pre_mix=g_pre_mix, w_in=w_in, sb_bias=sb_bias,
                   ssm_lam_re=ssm_lam_re, ssm_lam_im=ssm_lam_im, ssm_log_dt=ssm_log_dt,
                   ssm_b_re=ssm_b_re, ssm_b_im=ssm_b_im, ssm_c_re=ssm_c_re, ssm_c_im=ssm_c_im,
                   ssm_d=ssm_d, w_glu=w_glu, b_glu=b_glu, g_attn_out=g_attn_out,
                   g_ssm_out=g_ssm_out, w_out=w_out, g_post_mix=g_post_mix,
                   g_pre_ffn=g_pre_ffn, w_up=w_up, w_down=w_down, g_post_ffn=g_post_ffn)
    depth = w_in.shape[0]
    assert x_sample.shape[1] == 1, "decode path handles one new token per sequence"
    xp, xs = x_prompt, x_sample
    outs = []
    for l in range(depth):
        p = {n: a[l] for n, a in weights.items()}
        res = _layer(xp, xs, c_prompt, c_sample, cache_k[l], cache_v[l],
                     state_ssm_re[l], state_ssm_im[l], page_table, p)
        xp, xs = res[0], res[1]
        outs.append(res[2:])
    stacked = tuple(jnp.stack([o[i] for o in outs]) for i in range(8))
    return (xp, xs) + stacked
```

```python
import functools
import math

import numpy as np
import jax
import jax.numpy as jnp
from jax import lax
from jax.experimental import pallas as pl
from jax.experimental.pallas import tpu as pltpu

F32 = jnp.float32
BF16 = jnp.bfloat16
HIGHEST = lax.Precision.HIGHEST

RMS_EPS = 1e-6
HEAD_DIM = 64
SSM_GROUP = 16
SSM_STATE = 64
LOG2E = 1.4426950408889634
LN2 = 0.6931471805599453

LANES = 128
SUBLANES = 8
KEY_BLOCK = 128
Q_WIDTH = 512
PROJ_ROWS = 512
SSM_ROWS = 256
PAGES_PER_STEP = 16
SSM_UNIT = 16
VMEM_LIMIT = 56 * 1024 * 1024


def _cparams(n_axes):
    return pltpu.CompilerParams(dimension_semantics=("arbitrary",) * n_axes,
                                vmem_limit_bytes=VMEM_LIMIT)


def _rms(x, g):
    inv = lax.rsqrt(jnp.mean(x * x, axis=-1, keepdims=True) + RMS_EPS)
    return (x * inv) * g


def _dot(a, b, precise):
    if precise:
        return jnp.dot(a, b, precision=HIGHEST, preferred_element_type=F32)
    return jnp.dot(a.astype(BF16), b.astype(BF16), preferred_element_type=F32)


def _softplus2(z2):
    neg_abs = lax.bitcast_convert_type(
        lax.bitcast_convert_type(z2, jnp.uint32) | jnp.uint32(0x80000000), F32)
    e = jnp.exp2(neg_abs)
    return jnp.maximum(z2, 0.0) + jnp.log(1.0 + e) * LOG2E


def _split_bf16(p):
    hi = lax.bitcast_convert_type(
        lax.bitcast_convert_type(p, jnp.uint32) & jnp.uint32(0xFFFF0000), F32)
    return hi.astype(BF16), (p - hi).astype(BF16)


def _prep_kernel(lre_ref, lim_ref, dt_ref, bre_ref, bim_ref, lre8_ref, lim8_ref, dt8_ref,
                 bbre_ref, bbim_ref, tab_ref):
    def lam_bar(lre, lim, log_dt):
        dt = jnp.exp(log_dt)
        mag = jnp.exp(lre * dt)
        ang = lim * dt
        return mag * jnp.cos(ang), mag * jnp.sin(ang)

    lre, lim = lre_ref[...], lim_ref[...]
    lbr, lbi = lam_bar(lre, lim, dt_ref[...])
    nr, ni = lbr - 1.0, lbi
    den = lre * lre + lim * lim
    cre = (nr * lre + ni * lim) / den
    cim = (ni * lre - nr * lim) / den
    bre, bim = bre_ref[...], bim_ref[...]
    bbre_ref[...] = cre * bre - cim * bim
    bbim_ref[...] = cre * bim + cim * bre

    p1r, p1i = lam_bar(lre8_ref[...], lim8_ref[...], dt8_ref[...])
    row = lax.broadcasted_iota(jnp.int32, p1r.shape, 0)

    def cmul(ar, ai, br, bi):
        return ar * br - ai * bi, ar * bi + ai * br

    p2r, p2i = cmul(p1r, p1i, p1r, p1i)
    p4r, p4i = cmul(p2r, p2i, p2r, p2i)
    p8r, p8i = cmul(p4r, p4i, p4r, p4i)
    e = row + 1
    accr, acci = jnp.ones_like(p1r), jnp.zeros_like(p1r)
    for bit, (pr, pi) in enumerate(((p1r, p1i), (p2r, p2i), (p4r, p4i), (p8r, p8i))):
        on = ((e >> bit) & 1) == 1
        nr_, ni_ = cmul(accr, acci, pr, pi)
        accr = jnp.where(on, nr_, accr)
        acci = jnp.where(on, ni_, acci)
    zero = jnp.zeros_like(p1r)
    tab_ref[0] = jnp.where(row >= 1, p1r, zero)
    tab_ref[1] = jnp.where(row >= 1, p1i, zero)
    tab_ref[2] = jnp.where(row >= 2, p2r, zero)
    tab_ref[3] = jnp.where(row >= 2, p2i, zero)
    tab_ref[4] = jnp.where(row >= 4, p4r, zero)
    tab_ref[5] = jnp.where(row >= 4, p4i, zero)
    tab_ref[6] = accr
    tab_ref[7] = acci


def _prep(lam_re, lam_im, log_dt, b_re, b_im):
    g, p = lam_re.shape
    c = b_re.shape[-1]
    rows = g * c

    def rep(a):
        return jnp.broadcast_to(a[:, None, :], (g, c, p)).reshape(rows, p)

    def flat8(a):
        return jnp.broadcast_to(a.reshape(1, g * p), (SUBLANES, g * p))

    dt_gp = jnp.broadcast_to(log_dt[:, None], (g, p))
    bt = lambda b: b.transpose(0, 2, 1).reshape(rows, p)
    bbre, bbim, tab = pl.pallas_call(
        _prep_kernel,
        out_shape=(jax.ShapeDtypeStruct((rows, p), F32),
                   jax.ShapeDtypeStruct((rows, p), F32),
                   jax.ShapeDtypeStruct((8, SUBLANES, g * p), F32)),
        name="prep",
    )(rep(lam_re), rep(lam_im), rep(dt_gp), bt(b_re), bt(b_im),
      flat8(lam_re), flat8(lam_im), flat8(dt_gp))
    return bbre.reshape(g, c, p), bbim.reshape(g, c, p), tab


def _ada_kernel(c_ref, w_ref, b_ref, o_ref):
    c = c_ref[...]
    s = c * jax.nn.sigmoid(c)
    o_ref[...] = _dot(s, w_ref[...], True) + b_ref[...]


def _ada(c_all, w_ada, b_ada):
    rows, d = c_all.shape
    n = w_ada.shape[1]
    bn = 1024
    return pl.pallas_call(
        _ada_kernel,
        grid=(n // bn,),
        in_specs=[pl.BlockSpec((rows, d), lambda j: (0, 0)),
                  pl.BlockSpec((d, bn), lambda j: (0, j)),
                  pl.BlockSpec((1, bn), lambda j: (0, j))],
        out_specs=pl.BlockSpec((rows, bn), lambda j: (0, j)),
        out_shape=jax.ShapeDtypeStruct((rows, n), F32),
        compiler_params=_cparams(1),
        name="ada",
    )(c_all, w_ada, b_ada.reshape(1, n))


def _proj_kernel(x_ref, sh_ref, sc_ref, g_ref, w_ref, *out_refs, attn_w, precise, q_scale):
    h = _rms(x_ref[...], g_ref[...]) * (1.0 + sc_ref[...]) + sh_ref[...]
    p = _dot(h, w_ref[...], precise)
    q, k, v, u = (p[:, i * attn_w:(i + 1) * attn_w] for i in range(4))
    if precise:
        q_ref, k_ref, v_ref, u_ref = out_refs
        q_ref[...] = q * q_scale
    else:
        q_ref, k_ref, v_ref, u_ref, kb_ref, vb_ref = out_refs
        q_ref[...] = (q * q_scale).astype(BF16)
        kb_ref[...] = k.astype(BF16)
        vb_ref[...] = v.astype(BF16)
    k_ref[...] = k
    v_ref[...] = v
    u_ref[...] = u


def _proj_prompt(x, mod, g_pre, w_in_bf, q_scale):
    b, t, d = x.shape
    n = w_in_bf.shape[1]
    aw = n // 4
    tb = min(PROJ_ROWS, t)
    row = lambda bi, i: (bi, i, 0)
    modspec = lambda which: pl.BlockSpec((None, None, 1, d), lambda bi, i: (bi, which, 0, 0))
    f32o = jax.ShapeDtypeStruct((b, t, aw), F32)
    bfo = jax.ShapeDtypeStruct((b, t, aw), BF16)
    ospec = pl.BlockSpec((None, tb, aw), row)
    return pl.pallas_call(
        functools.partial(_proj_kernel, attn_w=aw, precise=False, q_scale=q_scale),
        grid=(b, t // tb),
        in_specs=[pl.BlockSpec((None, tb, d), row), modspec(0), modspec(1),
                  pl.BlockSpec((1, d), lambda bi, i: (0, 0)),
                  pl.BlockSpec((d, n), lambda bi, i: (0, 0))],
        out_specs=(ospec,) * 6,
        out_shape=(bfo, f32o, f32o, f32o, bfo, bfo),
        compiler_params=_cparams(2),
        name="proj_prompt",
    )(x, mod, mod, g_pre, w_in_bf)


def _proj_sample(x, sh, sc, g_pre, w_in, q_scale):
    rows, d = x.shape
    n = w_in.shape[1]
    aw = n // 4
    full = lambda shape: pl.BlockSpec(shape, lambda i: (0,) * len(shape))
    o = jax.ShapeDtypeStruct((rows, aw), F32)
    return pl.pallas_call(
        functools.partial(_proj_kernel, attn_w=aw, precise=True, q_scale=q_scale),
        grid=(1,),
        in_specs=[full((rows, d)), full((rows, d)), full((rows, d)), full((1, d)), full((d, n))],
        out_specs=(full((rows, aw)),) * 4,
        out_shape=(o, o, o, o),
        compiler_params=_cparams(1),
        name="proj_sample",
    )(x, sh, sc, g_pre, w_in)


BF16_TILE_ROWS = 16
NU_DIM = KEY_BLOCK + BF16_TILE_ROWS
CARRY_ROWS = 3
MASKED = -1e30
PIPE = 4


def _suffix_matrix():
    s = np.arange(NU_DIM)[:, None]
    j = np.arange(NU_DIM)[None, :]
    key = j < KEY_BLOCK
    m = np.where(key & ((j >= s) | (s >= KEY_BLOCK)), -1.0, 0.0)
    m = m + np.where((s < KEY_BLOCK) & (j >= KEY_BLOCK) & (j < KEY_BLOCK + CARRY_ROWS), 1.0, 0.0)
    return jnp.asarray(m, dtype=BF16)


def _attn_p_kernel(bias_ref, qt_ref, k_ref, vt_ref, nu_ref, o_ref,
                   qm_s, bm_s, z_s, p_s, tot_s, a_s, acc_s, carry_s):
    hp = pl.program_id(1)
    qi = pl.program_id(2)
    qw = qt_ref.shape[1]
    n_chunks = qw // LANES
    bpq = qw // KEY_BLOCK
    n_heads = bias_ref.shape[0] // 3
    zeros = lambda ref, *idx: jnp.zeros(ref.shape[len(idx):], ref.dtype)

    @pl.when(qi == 0)
    def _():
        s_io = lax.broadcasted_iota(jnp.int32, (KEY_BLOCK, qw), 0)
        t_io = lax.broadcasted_iota(jnp.int32, (KEY_BLOCK, qw), 1)
        for hh in range(2):
            bias2 = bias_ref[2 * hp + hh]
            bm_s[hh, 0] = jnp.full((KEY_BLOCK, qw), bias2, F32)
            for rel in range(bpq):
                vis = s_io + rel * KEY_BLOCK < t_io
                bm_s[hh, rel + 1] = jnp.where(vis, bias2, MASKED)
            bm_s[hh, bpq + 1] = jnp.full((KEY_BLOCK, qw), MASKED, F32)

    q32 = qt_ref[...].astype(F32)
    row = lax.broadcasted_iota(jnp.int32, q32.shape, 0)
    for hh in range(2):
        own = (row >= hh * HEAD_DIM) & (row < (hh + 1) * HEAD_DIM)
        qm_s[hh] = jnp.where(own, q32, 0.0).astype(BF16)
        acc_s[hh] = zeros(acc_s, hh)
        carry_s[hh] = zeros(carry_s, hh)
        z_s[hh, 2] = jnp.full(z_s.shape[2:], MASKED, F32)
        z_s[hh, 3] = jnp.full(z_s.shape[2:], MASKED, F32)
        p_s[hh, 0] = zeros(p_s, hh, 0)
        tot_s[hh, 1] = zeros(tot_s, hh, 1)
        a_s[hh, 1] = zeros(a_s, hh, 1)

    last = (qi + 1) * bpq - 1
    row8 = lax.broadcasted_iota(jnp.int32, (SUBLANES, qw), 0)
    top16 = lambda x: lax.bitcast_convert_type(
        lax.bitcast_convert_type(x, jnp.uint32) & jnp.uint32(0xFFFF0000), F32)

    def stage(i, u):
        e, o = u % 2, (u + 1) % 2
        j_in = jnp.maximum(last - i, 0)
        k_in = k_ref[pl.ds(pl.multiple_of(j_in * KEY_BLOCK, KEY_BLOCK), KEY_BLOCK), :]
        m_in = jnp.where(i > last, bpq + 1, jnp.clip(j_in - qi * bpq, -1, bpq - 1) + 1)
        j_out = jnp.clip(last - (i - 3), 0, last)
        for hh in range(2):
            z_s[hh, u] = jnp.dot(k_in, qm_s[hh], preferred_element_type=F32) + bm_s[hh, m_in]
        for hh in range(2):
            vt = vt_ref[j_out, hh * HEAD_DIM:(hh + 1) * HEAD_DIM, :]
            acc_s[hh] += jnp.dot(vt, a_s[hh, o], preferred_element_type=F32)
        r = []
        for hh in range(2):
            carry = carry_s[hh] + tot_s[hh, o]
            carry_s[hh] = carry
            shifted = carry - bias_ref[n_heads + 2 * hp + hh]
            c_hi = top16(shifted)
            c_mid = top16(shifted - c_hi)
            c_lo = (shifted - c_hi) - c_mid
            addends = jnp.where(row8 == 0, c_hi, jnp.where(row8 == 1, c_mid,
                                                           jnp.where(row8 == 2, c_lo, 0.0)))
            p_s[hh, e, KEY_BLOCK:NU_DIM, :] = jnp.concatenate(
                [addends, jnp.zeros_like(addends)], axis=0).astype(BF16)
            r.append(jnp.dot(nu_ref[...], p_s[hh, e], preferred_element_type=F32))
            tot_s[hh, e] = r[hh][KEY_BLOCK:KEY_BLOCK + SUBLANES, :]
        for hh in range(2):
            for c in range(n_chunks):
                sl = slice(c * LANES, (c + 1) * LANES)
                p_s[hh, o, 0:KEY_BLOCK, sl] = _softplus2(z_s[hh, (u - 1) % PIPE, :, sl]).astype(BF16)
        for hh in range(2):
            for c in range(n_chunks):
                sl = slice(c * LANES, (c + 1) * LANES)
                log_a = z_s[hh, (u - 2) % PIPE, :, sl] + r[hh][0:KEY_BLOCK, sl]
                a_s[hh, e, :, sl] = jnp.exp2(log_a.astype(BF16))

    def body(n, c):
        for u in range(PIPE):
            stage(PIPE * n + u, u)
        return c

    lax.fori_loop(0, ((qi + 1) * bpq + 3 + PIPE - 1) // PIPE, body, 0)

    for hh in range(2):
        o_ref[hh * HEAD_DIM:(hh + 1) * HEAD_DIM, :] = (
            acc_s[hh] * bias_ref[2 * n_heads + 2 * hp + hh])


def _attn_prompt(bias2, qt, kb, vb):
    b, w, t = qt.shape
    offset = jnp.round(bias2)
    head_scalars = jnp.concatenate([bias2, offset, jnp.exp2(offset)])
    qw = Q_WIDTH
    bpq = qw // KEY_BLOCK
    assert t % qw == 0 and bpq % 2 == 0
    pair = 2 * HEAD_DIM
    nkb = t // KEY_BLOCK
    vt = vb.reshape(b, nkb, KEY_BLOCK, w).transpose(0, 1, 3, 2)
    return pl.pallas_call(
        _attn_p_kernel,
        grid=(b, w // pair, t // qw),
        in_specs=[pl.BlockSpec(memory_space=pltpu.SMEM),
                  pl.BlockSpec((None, pair, qw), lambda bi, hp, qi: (bi, hp, qi)),
                  pl.BlockSpec((None, t, pair), lambda bi, hp, qi: (bi, 0, hp)),
                  pl.BlockSpec((None, nkb, pair, KEY_BLOCK), lambda bi, hp, qi: (bi, 0, hp, 0)),
                  pl.BlockSpec((NU_DIM, NU_DIM), lambda bi, hp, qi: (0, 0))],
        out_specs=pl.BlockSpec((None, pair, qw), lambda bi, hp, qi: (bi, hp, qi)),
        out_shape=jax.ShapeDtypeStruct((b, w, t), F32),
        scratch_shapes=[pltpu.VMEM((2, pair, qw), BF16),
                        pltpu.VMEM((2, bpq + 2, KEY_BLOCK, qw), F32),
                        pltpu.VMEM((2, PIPE, KEY_BLOCK, qw), F32),
                        pltpu.VMEM((2, 2, NU_DIM, qw), BF16),
                        pltpu.VMEM((2, 2, SUBLANES, qw), F32),
                        pltpu.VMEM((2, 2, KEY_BLOCK, qw), BF16),
                        pltpu.VMEM((2, HEAD_DIM, qw), F32),
                        pltpu.VMEM((2, SUBLANES, qw), F32)],
        compiler_params=_cparams(3),
        name="attn_prompt",
    )(head_scalars, qt, kb, vt, _suffix_matrix())


def _neg_suffix_matrix_lanes():
    j = np.arange(2 * KEY_BLOCK)[:, None] % KEY_BLOCK
    s = np.arange(2 * KEY_BLOCK)[None, :]
    return jnp.asarray(np.where((j >= s) | (s >= KEY_BLOCK), -1.0, 0.0), dtype=BF16)


def _attn_s_kernel(pt_ref, q_ref, kn_ref, vn_ref, bias_ref, nu_ref, *refs):
    pages = PAGES_PER_STEP
    k_refs, v_refs = refs[:pages], refs[pages:2 * pages]
    o_ref = refs[2 * pages]
    qx_s, acc_s, carry_s = refs[2 * pages + 1:]
    g = pl.program_id(1)
    rows, w = qx_s.shape
    lane_head = lax.broadcasted_iota(jnp.int32, (rows, w), 1) // HEAD_DIM
    own = lane_head == lax.broadcasted_iota(jnp.int32, (rows, w), 0)

    @pl.when(g == 0)
    def _():
        qx = jnp.where(own, jnp.broadcast_to(q_ref[0], (rows, w)), 0.0)
        qx_s[...] = qx.astype(BF16)
        tpos = lax.broadcasted_iota(jnp.int32, (rows, 1), 1)
        vis = tpos < tpos
        z_new = jnp.sum(qx * kn_ref[0], axis=-1, keepdims=True) + bias_ref[...]
        sp = _softplus2(z_new)
        log_1m = jnp.where(vis, -sp, 0.0)
        a_new = jnp.where(vis, jnp.exp2(z_new - sp), 0.0)
        acc_s[...] = a_new * jnp.where(own, jnp.broadcast_to(vn_ref[0], (rows, w)), 0.0)
        carry_s[...] = jnp.broadcast_to(log_1m, carry_s.shape)

    lanes_of = lambda x, r: x[:, r * KEY_BLOCK:(r + 1) * KEY_BLOCK]
    side_by_side = lambda page_refs: jnp.concatenate(
        [ref[0].astype(BF16) for ref in page_refs], axis=1)
    z = jnp.dot(qx_s[...], side_by_side(k_refs),
                preferred_element_type=F32) + bias_ref[...]
    hi, lo = _split_bf16(jnp.concatenate([_softplus2(lanes_of(z, r)) for r in range(pages)],
                                         axis=0))
    rr = jnp.dot(jnp.concatenate([hi, lo], axis=1), nu_ref[...],
                 preferred_element_type=F32)
    carry = carry_s[...]
    weights = []
    for r in range(pages):
        blk = rr[r * rows:(r + 1) * rows]
        weights.append(jnp.exp2(lanes_of(z, r) + blk[:, :KEY_BLOCK] + carry))
        carry = carry + blk[:, KEY_BLOCK:]
    carry_s[...] = carry
    a = jnp.concatenate(weights, axis=1).astype(BF16)
    acc_s[...] += lax.dot_general(a, side_by_side(v_refs), (((1,), (1,)), ((), ())),
                                  preferred_element_type=F32)

    @pl.when(g == pl.num_programs(1) - 1)
    def _():
        o_ref[0] = jnp.sum(jnp.where(own, acc_s[...], 0.0), axis=0, keepdims=True)


def _attn_sample(page_table, q2, k_new, v_new, bias2, pool_kt, pool_vt):
    db, w = q2.shape
    n_pool, _, page_rows = pool_kt.shape
    n_heads = w // HEAD_DIM
    n_pages = page_table.shape[1]
    pages = PAGES_PER_STEP
    assert n_pages % pages == 0 and page_rows == KEY_BLOCK and n_heads <= SUBLANES
    rows = BF16_TILE_ROWS
    bias_col = jnp.zeros((rows, 1), F32).at[:n_heads, 0].set(bias2)
    tok = lambda a: a.reshape(db, 1, w)
    tokspec = pl.BlockSpec((1, 1, w), lambda s, g, pt: (s, 0, 0))

    def page_spec(r):
        return pl.BlockSpec((1, w, page_rows),
                            lambda s, g, pt: (pt[s, n_pages - 1 - (g * pages + r)], 0, 0))

    grid_spec = pltpu.PrefetchScalarGridSpec(
        num_scalar_prefetch=1,
        grid=(db, n_pages // pages),
        in_specs=[tokspec, tokspec, tokspec,
                  pl.BlockSpec((rows, 1), lambda s, g, pt: (0, 0)),
                  pl.BlockSpec((2 * KEY_BLOCK, 2 * KEY_BLOCK), lambda s, g, pt: (0, 0))]
                 + [page_spec(r) for r in range(pages)] * 2,
        out_specs=tokspec,
        scratch_shapes=[pltpu.VMEM((rows, w), BF16),
                        pltpu.VMEM((rows, w), F32),
                        pltpu.VMEM((rows, KEY_BLOCK), F32)])
    out = pl.pallas_call(
        _attn_s_kernel,
        grid_spec=grid_spec,
        out_shape=jax.ShapeDtypeStruct((db, 1, w), F32),
        compiler_params=_cparams(2),
        name="attn_sample",
    )(page_table, tok(q2), tok(k_new), tok(v_new), bias_col, _neg_suffix_matrix_lanes(),
      *([pool_kt] * pages), *([pool_vt] * pages))
    return out.reshape(db, w)


def _gelu_glu(y, wg_ref, bg_ref, precise):
    g = 0.5 * y * (1.0 + jnp.tanh(math.sqrt(2.0 / math.pi) * (y + 0.044715 * (y * y * y))))
    return g * jax.nn.sigmoid(_dot(g, wg_ref[...], precise) + bg_ref[...])


def _ssm_p_kernel(u_ref, w_ref, cm_ref, tab_ref, d_ref, wg_ref, bg_ref,
                  o_ref, hre_ref, him_ref, bu_s, h_s):
    step = pl.program_id(0)
    nb, tc, width = u_ref.shape
    n_units = w_ref.shape[0]
    uw = width // n_units
    sw = w_ref.shape[2] // 2
    n_tiles = tc // SUBLANES

    @pl.when(step == 0)
    def _():
        h_s[:, :, 0:SUBLANES, :] = jnp.zeros((nb, n_units, SUBLANES, 2 * sw), F32)

    @pl.when(step > 0)
    def _():
        h_s[:, :, 0:SUBLANES, :] = h_s[:, :, tc:tc + SUBLANES, :]

    for b in range(nb):
        for m in range(n_units):
            bu_s[b, m] = _dot(u_ref[b, :, m * uw:(m + 1) * uw], w_ref[m], False)

    def tile_body(tb, carry):
        r0 = pl.multiple_of(tb * SUBLANES, SUBLANES)
        for b in range(nb):
            for m in range(n_units):
                for c in range(sw // LANES):
                    re = slice(c * LANES, (c + 1) * LANES)
                    im = slice(sw + c * LANES, sw + (c + 1) * LANES)
                    tl = slice(m * sw + c * LANES, m * sw + (c + 1) * LANES)
                    xr = bu_s[b, m, pl.ds(r0, SUBLANES), re]
                    xi = bu_s[b, m, pl.ds(r0, SUBLANES), im]
                    for k, ti in ((1, 0), (2, 2), (4, 4)):
                        ar, ai = tab_ref[ti, :, tl], tab_ref[ti + 1, :, tl]
                        sr = pltpu.roll(xr, k, 0)
                        si = pltpu.roll(xi, k, 0)
                        xr, xi = xr + (ar * sr - ai * si), xi + (ar * si + ai * sr)
                    prev_r = h_s[b, m, pl.ds(r0, SUBLANES), re]
                    prev_i = h_s[b, m, pl.ds(r0, SUBLANES), im]
                    pr = jnp.broadcast_to(prev_r[SUBLANES - 1:SUBLANES, :], (SUBLANES, LANES))
                    pi = jnp.broadcast_to(prev_i[SUBLANES - 1:SUBLANES, :], (SUBLANES, LANES))
                    lr, li = tab_ref[6, :, tl], tab_ref[7, :, tl]
                    xr, xi = xr + (lr * pr - li * pi), xi + (lr * pi + li * pr)
                    h_s[b, m, pl.ds(r0 + SUBLANES, SUBLANES), re] = xr
                    h_s[b, m, pl.ds(r0 + SUBLANES, SUBLANES), im] = xi
        return carry

    lax.fori_loop(0, n_tiles, tile_body, 0)

    for b in range(nb):
        ys = [_dot(h_s[b, m, SUBLANES:SUBLANES + tc, :], cm_ref[m], False) for m in range(n_units)]
        y = jnp.concatenate(ys, axis=1) + d_ref[...] * u_ref[b]
        o_ref[b] = _gelu_glu(y, wg_ref, bg_ref, False)
        for m in range(n_units):
            last = h_s[b, m, tc + SUBLANES - 1:tc + SUBLANES, :]
            hre_ref[b, :, m * sw:(m + 1) * sw] = last[:, :sw]
            him_ref[b, :, m * sw:(m + 1) * sw] = last[:, sw:]


def _ssm_prompt(u, w_units, c_units, tab, d_flat, w_glu_bf, b_glu):
    b, t, width = u.shape
    tc = min(SSM_ROWS, t)
    n_units, uw, sw2 = w_units.shape
    states = tab.shape[-1]
    const = lambda shape: pl.BlockSpec(shape, lambda i: (0,) * len(shape))
    return pl.pallas_call(
        _ssm_p_kernel,
        grid=(t // tc,),
        in_specs=[pl.BlockSpec((b, tc, width), lambda i: (0, i, 0)),
                  const(w_units.shape), const(c_units.shape), const(tab.shape),
                  const((1, width)), const(w_glu_bf.shape), const((1, width))],
        out_specs=(pl.BlockSpec((b, tc, width), lambda i: (0, i, 0)),
                   const((b, 1, states)), const((b, 1, states))),
        out_shape=(jax.ShapeDtypeStruct((b, t, width), F32),
                   jax.ShapeDtypeStruct((b, 1, states), F32),
                   jax.ShapeDtypeStruct((b, 1, states), F32)),
        scratch_shapes=[pltpu.VMEM((b, n_units, tc, sw2), F32),
                        pltpu.VMEM((b, n_units, tc + SUBLANES, sw2), F32)],
        compiler_params=_cparams(1),
        name="ssm_prompt",
    )(u, w_units, c_units, tab, d_flat, w_glu_bf, b_glu)


def _ssm_s_kernel(u_ref, h0r_ref, h0i_ref, lam_ref, wre_ref, wim_ref, cre_ref, cim_ref,
                  d_ref, wg_ref, bg_ref, o_ref, hre_ref, him_ref):
    u = u_ref[...]
    lr, li = lam_ref[0:1, :], lam_ref[1:2, :]
    h0r, h0i = h0r_ref[...], h0i_ref[...]
    hr = (lr * h0r - li * h0i) + _dot(u, wre_ref[...], True)
    hi = (lr * h0i + li * h0r) + _dot(u, wim_ref[...], True)
    hre_ref[...] = hr
    him_ref[...] = hi
    y = _dot(hr, cre_ref[...], True) - _dot(hi, cim_ref[...], True) + d_ref[...] * u
    o_ref[...] = _gelu_glu(y, wg_ref, bg_ref, True)


def _ssm_sample(u, h0r, h0i, lam2, wre, wim, cre, cim, d_flat, w_glu, b_glu):
    rows, width = u.shape
    states = h0r.shape[1]
    args = (u, h0r, h0i, lam2, wre, wim, cre, cim, d_flat, w_glu, b_glu)
    full = lambda a: pl.BlockSpec(a.shape, lambda i: (0,) * a.ndim)
    return pl.pallas_call(
        _ssm_s_kernel,
        grid=(1,),
        in_specs=[full(a) for a in args],
        out_specs=(pl.BlockSpec((rows, width), lambda i: (0, 0)),
                   pl.BlockSpec((rows, states), lambda i: (0, 0)),
                   pl.BlockSpec((rows, states), lambda i: (0, 0))),
        out_shape=(jax.ShapeDtypeStruct((rows, width), F32),
                   jax.ShapeDtypeStruct((rows, states), F32),
                   jax.ShapeDtypeStruct((rows, states), F32)),
        compiler_params=_cparams(1),
        name="ssm_sample",
    )(*args)


def _post_kernel(x_ref, oa_ref, os_ref, g1_ref, sh2_ref, sc2_ref, g2_ref,
                 ga_ref, gs_ref, gpm_ref, gpf_ref, gff_ref, wo_ref, wu_ref, wd_ref, y_ref,
                 *, precise):
    merged = jnp.concatenate([_rms(oa_ref[...], ga_ref[...]), _rms(os_ref[...], gs_ref[...])],
                             axis=-1)
    x1 = x_ref[...] + g1_ref[...] * _rms(_dot(merged, wo_ref[...], precise), gpm_ref[...])
    h = _rms(x1, gpf_ref[...]) * (1.0 + sc2_ref[...]) + sh2_ref[...]
    up = jnp.maximum(_dot(h, wu_ref[...], precise), 0.0)
    f = _dot(up * up, wd_ref[...], precise)
    y_ref[...] = x1 + g2_ref[...] * _rms(f, gff_ref[...])


def _post_prompt(x, o_attn, o_ssm, mod, gains, w_out, w_up, w_down):
    b, t, d = x.shape
    aw = o_attn.shape[-1]
    tb = min(PROJ_ROWS, t)
    row = lambda bi, i: (bi, i, 0)
    modspec = lambda which: pl.BlockSpec((None, None, 1, d), lambda bi, i: (bi, which, 0, 0))
    const = lambda a: pl.BlockSpec(a.shape, lambda bi, i: (0,) * a.ndim,
                                   pipeline_mode=pl.Buffered(1))
    return pl.pallas_call(
        functools.partial(_post_kernel, precise=False),
        grid=(b, t // tb),
        in_specs=[pl.BlockSpec((None, tb, d), row),
                  pl.BlockSpec((None, tb, aw), row), pl.BlockSpec((None, tb, aw), row),
                  modspec(2), modspec(3), modspec(4), modspec(5)]
                 + [const(g) for g in gains] + [const(w_out), const(w_up), const(w_down)],
        out_specs=pl.BlockSpec((None, tb, d), row),
        out_shape=jax.ShapeDtypeStruct((b, t, d), F32),
        compiler_params=_cparams(2),
        name="post_prompt",
    )(x, o_attn, o_ssm, mod, mod, mod, mod, *gains, w_out, w_up, w_down)


def _post_sample(x, o_attn, o_ssm, g1, sh2, sc2, g2, gains, w_out, w_up, w_down):
    rows, d = x.shape
    ff = w_up.shape[1]
    args = (x, o_attn, o_ssm, g1, sh2, sc2, g2, *gains, w_out, w_up, w_down)
    full = lambda a: pl.BlockSpec(a.shape, lambda i: (0,) * a.ndim, pipeline_mode=pl.Buffered(1))
    return pl.pallas_call(
        functools.partial(_post_kernel, precise=False),
        grid=(1,),
        in_specs=[full(a) for a in args],
        out_specs=pl.BlockSpec((rows, d), lambda i: (0, 0)),
        out_shape=jax.ShapeDtypeStruct((rows, d), F32),
        compiler_params=_cparams(1),
        name="post_sample",
    )(*args)


def _block_diag(blocks):
    g, r, c = blocks.shape
    eye = jnp.eye(g, dtype=blocks.dtype)
    return (blocks[:, :, None, :] * eye[:, None, :, None]).reshape(g * r, g * c)


def _layer(xp, xs, c_prompt, c_sample, pool_k, pool_v, h0_re, h0_im, page_table, p):
    b, t, d = xp.shape
    db = xs.shape[0]
    g, pstate = p["ssm_lam_re"].shape
    n_heads = p["sb_bias"].shape[0]
    attn_w = n_heads * HEAD_DIM
    ssm_w = g * SSM_GROUP
    states = g * pstate

    bbre, bbim, tab = _prep(p["ssm_lam_re"], p["ssm_lam_im"], p["ssm_log_dt"],
                            p["ssm_b_re"], p["ssm_b_im"])
    wre = _block_diag(bbre)
    wim = _block_diag(bbim)
    cre = _block_diag(p["ssm_c_re"].transpose(0, 2, 1))
    cim = _block_diag(p["ssm_c_im"].transpose(0, 2, 1))
    n_units = g // SSM_UNIT
    uw, sw = SSM_UNIT * SSM_GROUP, SSM_UNIT * pstate
    w_units = jnp.stack([jnp.concatenate([wre[m * uw:(m + 1) * uw, m * sw:(m + 1) * sw],
                                          wim[m * uw:(m + 1) * uw, m * sw:(m + 1) * sw]], axis=1)
                         for m in range(n_units)]).astype(BF16)
    c_units = jnp.stack([jnp.concatenate([cre[m * sw:(m + 1) * sw, m * uw:(m + 1) * uw],
                                          -cim[m * sw:(m + 1) * sw, m * uw:(m + 1) * uw]], axis=0)
                         for m in range(n_units)]).astype(BF16)
    d_flat = p["ssm_d"].reshape(1, ssm_w)
    b_glu = p["b_glu"].reshape(1, ssm_w)
    bias2 = p["sb_bias"] * LOG2E
    q_scale = HEAD_DIM ** -0.5 * LOG2E
    row1 = lambda a: a.reshape(1, -1)
    gains = [row1(p[n]) for n in ("g_attn_out", "g_ssm_out", "g_post_mix", "g_pre_ffn",
                                  "g_post_ffn")]

    n_mod = b + db
    pad = (-n_mod) % SUBLANES
    c_all = jnp.concatenate([c_prompt, c_sample, jnp.zeros((pad, d), F32)], axis=0)
    mod = _ada(c_all, p["w_ada"], p["b_ada"])
    mod_p = mod[:b].reshape(b, 6, 1, d)
    mod_s = mod[b:b + db].reshape(db, 6, d)

    w_in_bf = p["w_in"].astype(BF16)
    q2, k, v, u, kb, vb = _proj_prompt(xp, mod_p, row1(p["g_pre_mix"]), w_in_bf, q_scale)
    ot = _attn_prompt(bias2, q2.transpose(0, 2, 1), kb, vb)
    o_attn = ot.transpose(0, 2, 1)
    o_ssm, hre_p, him_p = _ssm_prompt(u, w_units, c_units, tab, d_flat,
                                      p["w_glu"].astype(BF16), b_glu)
    w_post = [p[n].astype(BF16) for n in ("w_out", "w_up", "w_down")]
    yp = _post_prompt(xp, o_attn, o_ssm, mod_p, gains, *w_post)

    xs2 = xs.reshape(db, d)
    qs, ks, vs, us = _proj_sample(xs2, mod_s[:, 0], mod_s[:, 1], row1(p["g_pre_mix"]),
                                  p["w_in"], q_scale)
    n_pool, page = pool_k.shape[:2]
    by_pos = lambda pool: pool.transpose(0, 2, 3, 1).reshape(n_pool, attn_w, page)
    oa_s = _attn_sample(page_table, qs, ks, vs, bias2, by_pos(pool_k), by_pos(pool_v))
    lam2 = jnp.concatenate([tab[6, 0:1], tab[7, 0:1]], axis=0)
    os_s, hre_s, him_s = _ssm_sample(us, h0_re.reshape(db, states), h0_im.reshape(db, states),
                                     lam2, wre, wim, cre, cim, d_flat, p["w_glu"], b_glu)
    ys = _post_sample(xs2, oa_s, os_s, mod_s[:, 2], mod_s[:, 3], mod_s[:, 4], mod_s[:, 5],
                      gains, *w_post)

    heads = (n_heads, HEAD_DIM)
    return (yp, ys.reshape(db, 1, d),
            k.reshape(b, t, *heads), v.reshape(b, t, *heads),
            hre_p.reshape(b, g, pstate), him_p.reshape(b, g, pstate),
            ks.reshape(db, 1, *heads), vs.reshape(db, 1, *heads),
            hre_s.reshape(db, g, pstate), him_s.reshape(db, g, pstate))


def kernel(x_prompt, x_sample, c_prompt, c_sample, cache_k, cache_v, state_ssm_re, state_ssm_im, page_table, w_ada, b_ada, g_pre_mix, w_in, sb_bias, ssm_lam_re, ssm_lam_im, ssm_log_dt, ssm_b_re, ssm_b_im, ssm_c_re, ssm_c_im, ssm_d, w_glu, b_glu, g_attn_out, g_ssm_out, w_out, g_post_mix, g_pre_ffn, w_up, w_down, g_post_ffn):
    weights = dict(w_ada=w_ada, b_ada=b_ada, g_pre_mix=g_pre_mix, w_in=w_in, sb_bias=sb_bias,
                   ssm_lam_re=ssm_lam_re, ssm_lam_im=ssm_lam_im, ssm_log_dt=ssm_log_dt,
                   ssm_b_re=ssm_b_re, ssm_b_im=ssm_b_im, ssm_c_re=ssm_c_re, ssm_c_im=ssm_c_im,
                   ssm_d=ssm_d, w_glu=w_glu, b_glu=b_glu, g_attn_out=g_attn_out,
                   g_ssm_out=g_ssm_out, w_out=w_out, g_post_mix=g_post_mix,
                   g_pre_ffn=g_pre_ffn, w_up=w_up, w_down=w_down, g_post_ffn=g_post_ffn)
    depth = w_in.shape[0]
    assert x_sample.shape[1] == 1, "decode path handles one new token per sequence"
    xp, xs = x_prompt, x_sample
    outs = []
    for l in range(depth):
        p = {n: a[l] for n, a in weights.items()}
        res = _layer(xp, xs, c_prompt, c_sample, cache_k[l], cache_v[l],
                     state_ssm_re[l], state_ssm_im[l], page_table, p)
        xp, xs = res[0], res[1]
        outs.append(res[2:])
    stacked = tuple(jnp.stack([o[i] for o in outs]) for i in range(8))
    return (xp, xs) + stacked
```

```python
import functools
import math

import numpy as np
import jax
import jax.numpy as jnp
from jax import lax
from jax.experimental import pallas as pl
from jax.experimental.pallas import tpu as pltpu

F32 = jnp.float32
BF16 = jnp.bfloat16
HIGHEST = lax.Precision.HIGHEST

RMS_EPS = 1e-6
HEAD_DIM = 64
SSM_GROUP = 16
SSM_STATE = 64
LOG2E = 1.4426950408889634
LN2 = 0.6931471805599453

LANES = 128
SUBLANES = 8
KEY_BLOCK = 128
Q_WIDTH = 512
PROJ_ROWS = 512
SSM_ROWS = 256
PAGES_PER_STEP = 16
SSM_UNIT = 16
VMEM_LIMIT = 56 * 1024 * 1024


def _cparams(n_axes):
    return pltpu.CompilerParams(dimension_semantics=("arbitrary",) * n_axes,
                                vmem_limit_bytes=VMEM_LIMIT)


def _rms(x, g):
    inv = lax.rsqrt(jnp.mean(x * x, axis=-1, keepdims=True) + RMS_EPS)
    return (x * inv) * g


def _dot(a, b, precise):
    if precise:
        return jnp.dot(a, b, precision=HIGHEST, preferred_element_type=F32)
    return jnp.dot(a.astype(BF16), b.astype(BF16), preferred_element_type=F32)


def _softplus2(z2):
    neg_abs = lax.bitcast_convert_type(
        lax.bitcast_convert_type(z2, jnp.uint32) | jnp.uint32(0x80000000), F32)
    e = jnp.exp2(neg_abs)
    return jnp.maximum(z2, 0.0) + jnp.log(1.0 + e) * LOG2E


def _split_bf16(p):
    hi = lax.bitcast_convert_type(
        lax.bitcast_convert_type(p, jnp.uint32) & jnp.uint32(0xFFFF0000), F32)
    return hi.astype(BF16), (p - hi).astype(BF16)


def _prep_kernel(lre_ref, lim_ref, dt_ref, bre_ref, bim_ref, lre8_ref, lim8_ref, dt8_ref,
                 bbre_ref, bbim_ref, tab_ref):
    def lam_bar(lre, lim, log_dt):
        dt = jnp.exp(log_dt)
        mag = jnp.exp(lre * dt)
        ang = lim * dt
        return mag * jnp.cos(ang), mag * jnp.sin(ang)

    lre, lim = lre_ref[...], lim_ref[...]
    lbr, lbi = lam_bar(lre, lim, dt_ref[...])
    nr, ni = lbr - 1.0, lbi
    den = lre * lre + lim * lim
    cre = (nr * lre + ni * lim) / den
    cim = (ni * lre - nr * lim) / den
    bre, bim = bre_ref[...], bim_ref[...]
    bbre_ref[...] = cre * bre - cim * bim
    bbim_ref[...] = cre * bim + cim * bre

    p1r, p1i = lam_bar(lre8_ref[...], lim8_ref[...], dt8_ref[...])
    row = lax.broadcasted_iota(jnp.int32, p1r.shape, 0)

    def cmul(ar, ai, br, bi):
        return ar * br - ai * bi, ar * bi + ai * br

    p2r, p2i = cmul(p1r, p1i, p1r, p1i)
    p4r, p4i = cmul(p2r, p2i, p2r, p2i)
    p8r, p8i = cmul(p4r, p4i, p4r, p4i)
    e = row + 1
    accr, acci = jnp.ones_like(p1r), jnp.zeros_like(p1r)
    for bit, (pr, pi) in enumerate(((p1r, p1i), (p2r, p2i), (p4r, p4i), (p8r, p8i))):
        on = ((e >> bit) & 1) == 1
        nr_, ni_ = cmul(accr, acci, pr, pi)
        accr = jnp.where(on, nr_, accr)
        acci = jnp.where(on, ni_, acci)
    zero = jnp.zeros_like(p1r)
    tab_ref[0] = jnp.where(row >= 1, p1r, zero)
    tab_ref[1] = jnp.where(row >= 1, p1i, zero)
    tab_ref[2] = jnp.where(row >= 2, p2r, zero)
    tab_ref[3] = jnp.where(row >= 2, p2i, zero)
    tab_ref[4] = jnp.where(row >= 4, p4r, zero)
    tab_ref[5] = jnp.where(row >= 4, p4i, zero)
    tab_ref[6] = accr
    tab_ref[7] = acci


def _prep(lam_re, lam_im, log_dt, b_re, b_im):
    g, p = lam_re.shape
    c = b_re.shape[-1]
    rows = g * c

    def rep(a):
        return jnp.broadcast_to(a[:, None, :], (g, c, p)).reshape(rows, p)

    def flat8(a):
        return jnp.broadcast_to(a.reshape(1, g * p), (SUBLANES, g * p))

    dt_gp = jnp.broadcast_to(log_dt[:, None], (g, p))
    bt = lambda b: b.transpose(0, 2, 1).reshape(rows, p)
    bbre, bbim, tab = pl.pallas_call(
        _prep_kernel,
        out_shape=(jax.ShapeDtypeStruct((rows, p), F32),
                   jax.ShapeDtypeStruct((rows, p), F32),
                   jax.ShapeDtypeStruct((8, SUBLANES, g * p), F32)),
        name="prep",
    )(rep(lam_re), rep(lam_im), rep(dt_gp), bt(b_re), bt(b_im),
      flat8(lam_re), flat8(lam_im), flat8(dt_gp))
    return bbre.reshape(g, c, p), bbim.reshape(g, c, p), tab


def _ada_kernel(c_ref, w_ref, b_ref, o_ref):
    c = c_ref[...]
    s = c * jax.nn.sigmoid(c)
    o_ref[...] = _dot(s, w_ref[...], True) + b_ref[...]


def _ada(c_all, w_ada, b_ada):
    rows, d = c_all.shape
    n = w_ada.shape[1]
    bn = 1024
    return pl.pallas_call(
        _ada_kernel,
        grid=(n // bn,),
        in_specs=[pl.BlockSpec((rows, d), lambda j: (0, 0)),
                  pl.BlockSpec((d, bn), lambda j: (0, j)),
                  pl.BlockSpec((1, bn), lambda j: (0, j))],
        out_specs=pl.BlockSpec((rows, bn), lambda j: (0, j)),
        out_shape=jax.ShapeDtypeStruct((rows, n), F32),
        compiler_params=_cparams(1),
        name="ada",
    )(c_all, w_ada, b_ada.reshape(1, n))


def _proj_kernel(x_ref, sh_ref, sc_ref, g_ref, w_ref, *out_refs, attn_w, precise, q_scale):
    h = _rms(x_ref[...], g_ref[...]) * (1.0 + sc_ref[...]) + sh_ref[...]
    p = _dot(h, w_ref[...], precise)
    q, k, v, u = (p[:, i * attn_w:(i + 1) * attn_w] for i in range(4))
    if precise:
        q_ref, k_ref, v_ref, u_ref = out_refs
        q_ref[...] = q * q_scale
    else:
        q_ref, k_ref, v_ref, u_ref, kb_ref, vb_ref = out_refs
        q_ref[...] = (q * q_scale).astype(BF16)
        kb_ref[...] = k.astype(BF16)
        vb_ref[...] = v.astype(BF16)
    k_ref[...] = k
    v_ref[...] = v
    u_ref[...] = u


def _proj_prompt(x, mod, g_pre, w_in_bf, q_scale):
    b, t, d = x.shape
    n = w_in_bf.shape[1]
    aw = n // 4
    tb = min(PROJ_ROWS, t)
    row = lambda bi, i: (bi, i, 0)
    modspec = lambda which: pl.BlockSpec((None, None, 1, d), lambda bi, i: (bi, which, 0, 0))
    f32o = jax.ShapeDtypeStruct((b, t, aw), F32)
    bfo = jax.ShapeDtypeStruct((b, t, aw), BF16)
    ospec = pl.BlockSpec((None, tb, aw), row)
    return pl.pallas_call(
        functools.partial(_proj_kernel, attn_w=aw, precise=False, q_scale=q_scale),
        grid=(b, t // tb),
        in_specs=[pl.BlockSpec((None, tb, d), row), modspec(0), modspec(1),
                  pl.BlockSpec((1, d), lambda bi, i: (0, 0)),
                  pl.BlockSpec((d, n), lambda bi, i: (0, 0))],
        out_specs=(ospec,) * 6,
        out_shape=(bfo, f32o, f32o, f32o, bfo, bfo),
        compiler_params=_cparams(2),
        name="proj_prompt",
    )(x, mod, mod, g_pre, w_in_bf)


def _proj_sample(x, sh, sc, g_pre, w_in, q_scale):
    rows, d = x.shape
    n = w_in.shape[1]
    aw = n // 4
    full = lambda shape: pl.BlockSpec(shape, lambda i: (0,) * len(shape))
    o = jax.ShapeDtypeStruct((rows, aw), F32)
    return pl.pallas_call(
        functools.partial(_proj_kernel, attn_w=aw, precise=True, q_scale=q_scale),
        grid=(1,),
        in_specs=[full((rows, d)), full((rows, d)), full((rows, d)), full((1, d)), full((d, n))],
        out_specs=(full((rows, aw)),) * 4,
        out_shape=(o, o, o, o),
        compiler_params=_cparams(1),
        name="proj_sample",
    )(x, sh, sc, g_pre, w_in)


BF16_TILE_ROWS = 16
NU_DIM = KEY_BLOCK + BF16_TILE_ROWS
CARRY_ROWS = 3
MASKED = -1e30
PIPE = 4
ROW_CHUNK = 16


def _suffix_matrix():
    s = np.arange(NU_DIM)[:, None] - BF16_TILE_ROWS
    j = np.arange(NU_DIM)[None, :]
    key = j < KEY_BLOCK
    m = np.where(key & ((j >= s) | (s < 0)), -1.0, 0.0)
    m = m + np.where((s >= 0) & (j >= KEY_BLOCK) & (j < KEY_BLOCK + CARRY_ROWS), 1.0, 0.0)
    return jnp.asarray(m, dtype=BF16)


def _attn_p_kernel(bias_ref, qt_ref, k_ref, vt_ref, nu_ref, o_ref,
                   qm_s, bm_s, z_s, p_s, tot_s, a_s, acc_s, carry_s):
    hp = pl.program_id(1)
    qi = pl.program_id(2)
    qw = qt_ref.shape[1]
    bpq = qw // KEY_BLOCK
    zeros = lambda ref, *idx: jnp.zeros(ref.shape[len(idx):], ref.dtype)

    @pl.when(qi == 0)
    def _():
        s_io = lax.broadcasted_iota(jnp.int32, (KEY_BLOCK, qw), 0)
        t_io = lax.broadcasted_iota(jnp.int32, (KEY_BLOCK, qw), 1)
        for hh in range(2):
            bias2 = bias_ref[2 * hp + hh]
            bm_s[hh, 0] = jnp.full((KEY_BLOCK, qw), bias2, F32)
            for rel in range(bpq):
                vis = s_io + rel * KEY_BLOCK < t_io
                bm_s[hh, rel + 1] = jnp.where(vis, bias2, MASKED)
            bm_s[hh, bpq + 1] = jnp.full((KEY_BLOCK, qw), MASKED, F32)

    q32 = qt_ref[...].astype(F32)
    row = lax.broadcasted_iota(jnp.int32, q32.shape, 0)
    for hh in range(2):
        own = (row >= hh * HEAD_DIM) & (row < (hh + 1) * HEAD_DIM)
        qm_s[hh] = jnp.where(own, q32, 0.0).astype(BF16)
        acc_s[hh] = zeros(acc_s, hh)
        carry_s[hh] = zeros(carry_s, hh)
        z_s[hh, 2] = jnp.full(z_s.shape[2:], MASKED, F32)
        z_s[hh, 3] = jnp.full(z_s.shape[2:], MASKED, F32)
        p_s[hh, 0] = zeros(p_s, hh, 0)
        tot_s[hh, 1] = zeros(tot_s, hh, 1)
        a_s[hh, 1] = zeros(a_s, hh, 1)

    last = (qi + 1) * bpq - 1
    row8 = lax.broadcasted_iota(jnp.int32, (SUBLANES, qw), 0)
    top16 = lambda x: lax.bitcast_convert_type(
        lax.bitcast_convert_type(x, jnp.uint32) & jnp.uint32(0xFFFF0000), F32)

    def stage(i, u):
        e, o = u % 2, (u + 1) % 2
        j_in = jnp.maximum(last - i, 0)
        k_in = k_ref[pl.ds(pl.multiple_of(j_in * KEY_BLOCK, KEY_BLOCK), KEY_BLOCK), :]
        m_in = jnp.where(i > last, bpq + 1, jnp.clip(j_in - qi * bpq, -1, bpq - 1) + 1)
        j_out = jnp.clip(last - (i - 3), 0, last)
        for hh in range(2):
            z_s[hh, u] = jnp.dot(k_in, qm_s[hh], preferred_element_type=F32) + bm_s[hh, m_in]
        for hh in range(2):
            vt = vt_ref[j_out, hh * HEAD_DIM:(hh + 1) * HEAD_DIM, :]
            acc_s[hh] += jnp.dot(vt, a_s[hh, o], preferred_element_type=F32)
        r = []
        for hh in range(2):
            carry = carry_s[hh] + tot_s[hh, o]
            carry_s[hh] = carry
            c_hi = top16(carry)
            c_mid = top16(carry - c_hi)
            c_lo = (carry - c_hi) - c_mid
            addends = jnp.where(row8 == 0, c_hi, jnp.where(row8 == 1, c_mid,
                                                           jnp.where(row8 == 2, c_lo, 0.0)))
            p_s[hh, e, KEY_BLOCK:NU_DIM, :] = jnp.concatenate(
                [addends, jnp.zeros_like(addends)], axis=0).astype(BF16)
            r.append(jnp.dot(nu_ref[...], p_s[hh, e], preferred_element_type=F32))
            tot_s[hh, e] = r[hh][0:SUBLANES, :]
        for hh in range(2):
            for c in range(KEY_BLOCK // ROW_CHUNK):
                sl = slice(c * ROW_CHUNK, (c + 1) * ROW_CHUNK)
                p_s[hh, o, sl, :] = _softplus2(z_s[hh, (u - 1) % PIPE, sl, :]).astype(BF16)
        for hh in range(2):
            for c in range(KEY_BLOCK // ROW_CHUNK):
                sl = slice(c * ROW_CHUNK, (c + 1) * ROW_CHUNK)
                log_a = (z_s[hh, (u - 2) % PIPE, sl, :]
                         + r[hh][BF16_TILE_ROWS + c * ROW_CHUNK:BF16_TILE_ROWS + (c + 1) * ROW_CHUNK, :])
                a_s[hh, e, sl, :] = jnp.exp2(log_a).astype(BF16)

    def body(n, c):
        for u in range(PIPE):
            stage(PIPE * n + u, u)
        return c

    lax.fori_loop(0, ((qi + 1) * bpq + 3 + PIPE - 1) // PIPE, body, 0)

    for hh in range(2):
        o_ref[hh * HEAD_DIM:(hh + 1) * HEAD_DIM, :] = acc_s[hh]


def _attn_prompt(bias2, qt, kb, vb):
    b, w, t = qt.shape
    qw = Q_WIDTH
    bpq = qw // KEY_BLOCK
    assert t % qw == 0 and bpq % 2 == 0
    pair = 2 * HEAD_DIM
    nkb = t // KEY_BLOCK
    vt = vb.reshape(b, nkb, KEY_BLOCK, w).transpose(0, 1, 3, 2)
    return pl.pallas_call(
        _attn_p_kernel,
        grid=(b, w // pair, t // qw),
        in_specs=[pl.BlockSpec(memory_space=pltpu.SMEM),
                  pl.BlockSpec((None, pair, qw), lambda bi, hp, qi: (bi, hp, qi)),
                  pl.BlockSpec((None, t, pair), lambda bi, hp, qi: (bi, 0, hp)),
                  pl.BlockSpec((None, nkb, pair, KEY_BLOCK), lambda bi, hp, qi: (bi, 0, hp, 0)),
                  pl.BlockSpec((NU_DIM, NU_DIM), lambda bi, hp, qi: (0, 0))],
        out_specs=pl.BlockSpec((None, pair, qw), lambda bi, hp, qi: (bi, hp, qi)),
        out_shape=jax.ShapeDtypeStruct((b, w, t), F32),
        scratch_shapes=[pltpu.VMEM((2, pair, qw), BF16),
                        pltpu.VMEM((2, bpq + 2, KEY_BLOCK, qw), F32),
                        pltpu.VMEM((2, PIPE, KEY_BLOCK, qw), F32),
                        pltpu.VMEM((2, 2, NU_DIM, qw), BF16),
                        pltpu.VMEM((2, 2, SUBLANES, qw), F32),
                        pltpu.VMEM((2, 2, KEY_BLOCK, qw), BF16),
                        pltpu.VMEM((2, HEAD_DIM, qw), F32),
                        pltpu.VMEM((2, SUBLANES, qw), F32)],
        compiler_params=_cparams(3),
        name="attn_prompt",
    )(bias2, qt, kb, vt, _suffix_matrix())


def _neg_suffix_matrix_lanes():
    j = np.arange(2 * KEY_BLOCK)[:, None] % KEY_BLOCK
    s = np.arange(2 * KEY_BLOCK)[None, :]
    return jnp.asarray(np.where((j >= s) | (s >= KEY_BLOCK), -1.0, 0.0), dtype=BF16)


def _attn_s_kernel(pt_ref, q_ref, kn_ref, vn_ref, bias_ref, nu_ref, *refs):
    pages = PAGES_PER_STEP
    k_refs, v_refs = refs[:pages], refs[pages:2 * pages]
    o_ref = refs[2 * pages]
    qx_s, acc_s, carry_s = refs[2 * pages + 1:]
    g = pl.program_id(1)
    rows, w = qx_s.shape
    lane_head = lax.broadcasted_iota(jnp.int32, (rows, w), 1) // HEAD_DIM
    own = lane_head == lax.broadcasted_iota(jnp.int32, (rows, w), 0)

    @pl.when(g == 0)
    def _():
        qx = jnp.where(own, jnp.broadcast_to(q_ref[0], (rows, w)), 0.0)
        qx_s[...] = qx.astype(BF16)
        tpos = lax.broadcasted_iota(jnp.int32, (rows, 1), 1)
        vis = tpos < tpos
        z_new = jnp.sum(qx * kn_ref[0], axis=-1, keepdims=True) + bias_ref[...]
        sp = _softplus2(z_new)
        log_1m = jnp.where(vis, -sp, 0.0)
        a_new = jnp.where(vis, jnp.exp2(z_new - sp), 0.0)
        acc_s[...] = a_new * jnp.where(own, jnp.broadcast_to(vn_ref[0], (rows, w)), 0.0)
        carry_s[...] = jnp.broadcast_to(log_1m, carry_s.shape)

    lanes_of = lambda x, r: x[:, r * KEY_BLOCK:(r + 1) * KEY_BLOCK]
    side_by_side = lambda page_refs: jnp.concatenate(
        [ref[0].astype(BF16) for ref in page_refs], axis=1)
    z = jnp.dot(qx_s[...], side_by_side(k_refs),
                preferred_element_type=F32) + bias_ref[...]
    hi, lo = _split_bf16(jnp.concatenate([_softplus2(lanes_of(z, r)) for r in range(pages)],
                                         axis=0))
    rr = jnp.dot(jnp.concatenate([hi, lo], axis=1), nu_ref[...],
                 preferred_element_type=F32)
    carry = carry_s[...]
    weights = []
    for r in range(pages):
        blk = rr[r * rows:(r + 1) * rows]
        weights.append(jnp.exp2(lanes_of(z, r) + blk[:, :KEY_BLOCK] + carry))
        carry = carry + blk[:, KEY_BLOCK:]
    carry_s[...] = carry
    a = jnp.concatenate(weights, axis=1).astype(BF16)
    acc_s[...] += lax.dot_general(a, side_by_side(v_refs), (((1,), (1,)), ((), ())),
                                  preferred_element_type=F32)

    @pl.when(g == pl.num_programs(1) - 1)
    def _():
        o_ref[0] = jnp.sum(jnp.where(own, acc_s[...], 0.0), axis=0, keepdims=True)


def _attn_sample(page_table, q2, k_new, v_new, bias2, pool_kt, pool_vt):
    db, w = q2.shape
    n_pool, _, page_rows = pool_kt.shape
    n_heads = w // HEAD_DIM
    n_pages = page_table.shape[1]
    pages = PAGES_PER_STEP
    assert n_pages % pages == 0 and page_rows == KEY_BLOCK and n_heads <= SUBLANES
    rows = BF16_TILE_ROWS
    bias_col = jnp.zeros((rows, 1), F32).at[:n_heads, 0].set(bias2)
    tok = lambda a: a.reshape(db, 1, w)
    tokspec = pl.BlockSpec((1, 1, w), lambda s, g, pt: (s, 0, 0))

    def page_spec(r):
        return pl.BlockSpec((1, w, page_rows),
                            lambda s, g, pt: (pt[s, n_pages - 1 - (g * pages + r)], 0, 0))

    grid_spec = pltpu.PrefetchScalarGridSpec(
        num_scalar_prefetch=1,
        grid=(db, n_pages // pages),
        in_specs=[tokspec, tokspec, tokspec,
                  pl.BlockSpec((rows, 1), lambda s, g, pt: (0, 0)),
                  pl.BlockSpec((2 * KEY_BLOCK, 2 * KEY_BLOCK), lambda s, g, pt: (0, 0))]
                 + [page_spec(r) for r in range(pages)] * 2,
        out_specs=tokspec,
        scratch_shapes=[pltpu.VMEM((rows, w), BF16),
                        pltpu.VMEM((rows, w), F32),
                        pltpu.VMEM((rows, KEY_BLOCK), F32)])
    out = pl.pallas_call(
        _attn_s_kernel,
        grid_spec=grid_spec,
        out_shape=jax.ShapeDtypeStruct((db, 1, w), F32),
        compiler_params=_cparams(2),
        name="attn_sample",
    )(page_table, tok(q2), tok(k_new), tok(v_new), bias_col, _neg_suffix_matrix_lanes(),
      *([pool_kt] * pages), *([pool_vt] * pages))
    return out.reshape(db, w)


def _gelu_glu(y, wg_ref, bg_ref, precise):
    g = 0.5 * y * (1.0 + jnp.tanh(math.sqrt(2.0 / math.pi) * (y + 0.044715 * (y * y * y))))
    return g * jax.nn.sigmoid(_dot(g, wg_ref[...], precise) + bg_ref[...])


def _ssm_p_kernel(u_ref, w_ref, cm_ref, tab_ref, d_ref, wg_ref, bg_ref,
                  o_ref, hre_ref, him_ref, bu_s, h_s):
    step = pl.program_id(0)
    nb, tc, width = u_ref.shape
    n_units = w_ref.shape[0]
    uw = width // n_units
    sw = w_ref.shape[2] // 2
    n_tiles = tc // SUBLANES

    @pl.when(step == 0)
    def _():
        h_s[:, :, 0:SUBLANES, :] = jnp.zeros((nb, n_units, SUBLANES, 2 * sw), F32)

    @pl.when(step > 0)
    def _():
        h_s[:, :, 0:SUBLANES, :] = h_s[:, :, tc:tc + SUBLANES, :]

    for b in range(nb):
        for m in range(n_units):
            bu_s[b, m] = _dot(u_ref[b, :, m * uw:(m + 1) * uw], w_ref[m], False)

    def tile_body(tb, carry):
        r0 = pl.multiple_of(tb * SUBLANES, SUBLANES)
        for b in range(nb):
            for m in range(n_units):
                for c in range(sw // LANES):
                    re = slice(c * LANES, (c + 1) * LANES)
                    im = slice(sw + c * LANES, sw + (c + 1) * LANES)
                    tl = slice(m * sw + c * LANES, m * sw + (c + 1) * LANES)
                    xr = bu_s[b, m, pl.ds(r0, SUBLANES), re]
                    xi = bu_s[b, m, pl.ds(r0, SUBLANES), im]
                    for k, ti in ((1, 0), (2, 2), (4, 4)):
                        ar, ai = tab_ref[ti, :, tl], tab_ref[ti + 1, :, tl]
                        sr = pltpu.roll(xr, k, 0)
                        si = pltpu.roll(xi, k, 0)
                        xr, xi = xr + (ar * sr - ai * si), xi + (ar * si + ai * sr)
                    prev_r = h_s[b, m, pl.ds(r0, SUBLANES), re]
                    prev_i = h_s[b, m, pl.ds(r0, SUBLANES), im]
                    pr = jnp.broadcast_to(prev_r[SUBLANES - 1:SUBLANES, :], (SUBLANES, LANES))
                    pi = jnp.broadcast_to(prev_i[SUBLANES - 1:SUBLANES, :], (SUBLANES, LANES))
                    lr, li = tab_ref[6, :, tl], tab_ref[7, :, tl]
                    xr, xi = xr + (lr * pr - li * pi), xi + (lr * pi + li * pr)
                    h_s[b, m, pl.ds(r0 + SUBLANES, SUBLANES), re] = xr
                    h_s[b, m, pl.ds(r0 + SUBLANES, SUBLANES), im] = xi
        return carry

    lax.fori_loop(0, n_tiles, tile_body, 0)

    for b in range(nb):
        ys = [_dot(h_s[b, m, SUBLANES:SUBLANES + tc, :], cm_ref[m], False) for m in range(n_units)]
        y = jnp.concatenate(ys, axis=1) + d_ref[...] * u_ref[b]
        o_ref[b] = _gelu_glu(y, wg_ref, bg_ref, False)
        for m in range(n_units):
            last = h_s[b, m, tc + SUBLANES - 1:tc + SUBLANES, :]
            hre_ref[b, :, m * sw:(m + 1) * sw] = last[:, :sw]
            him_ref[b, :, m * sw:(m + 1) * sw] = last[:, sw:]


def _ssm_prompt(u, w_units, c_units, tab, d_flat, w_glu_bf, b_glu):
    b, t, width = u.shape
    tc = min(SSM_ROWS, t)
    n_units, uw, sw2 = w_units.shape
    states = tab.shape[-1]
    const = lambda shape: pl.BlockSpec(shape, lambda i: (0,) * len(shape))
    return pl.pallas_call(
        _ssm_p_kernel,
        grid=(t // tc,),
        in_specs=[pl.BlockSpec((b, tc, width), lambda i: (0, i, 0)),
                  const(w_units.shape), const(c_units.shape), const(tab.shape),
                  const((1, width)), const(w_glu_bf.shape), const((1, width))],
        out_specs=(pl.BlockSpec((b, tc, width), lambda i: (0, i, 0)),
                   const((b, 1, states)), const((b, 1, states))),
        out_shape=(jax.ShapeDtypeStruct((b, t, width), F32),
                   jax.ShapeDtypeStruct((b, 1, states), F32),
                   jax.ShapeDtypeStruct((b, 1, states), F32)),
        scratch_shapes=[pltpu.VMEM((b, n_units, tc, sw2), F32),
                        pltpu.VMEM((b, n_units, tc + SUBLANES, sw2), F32)],
        compiler_params=_cparams(1),
        name="ssm_prompt",
    )(u, w_units, c_units, tab, d_flat, w_glu_bf, b_glu)


def _ssm_s_kernel(u_ref, h0r_ref, h0i_ref, lam_ref, wre_ref, wim_ref, cre_ref, cim_ref,
                  d_ref, wg_ref, bg_ref, o_ref, hre_ref, him_ref):
    u = u_ref[...]
    lr, li = lam_ref[0:1, :], lam_ref[1:2, :]
    h0r, h0i = h0r_ref[...], h0i_ref[...]
    hr = (lr * h0r - li * h0i) + _dot(u, wre_ref[...], True)
    hi = (lr * h0i + li * h0r) + _dot(u, wim_ref[...], True)
    hre_ref[...] = hr
    him_ref[...] = hi
    y = _dot(hr, cre_ref[...], True) - _dot(hi, cim_ref[...], True) + d_ref[...] * u
    o_ref[...] = _gelu_glu(y, wg_ref, bg_ref, True)


def _ssm_sample(u, h0r, h0i, lam2, wre, wim, cre, cim, d_flat, w_glu, b_glu):
    rows, width = u.shape
    states = h0r.shape[1]
    args = (u, h0r, h0i, lam2, wre, wim, cre, cim, d_flat, w_glu, b_glu)
    full = lambda a: pl.BlockSpec(a.shape, lambda i: (0,) * a.ndim)
    return pl.pallas_call(
        _ssm_s_kernel,
        grid=(1,),
        in_specs=[full(a) for a in args],
        out_specs=(pl.BlockSpec((rows, width), lambda i: (0, 0)),
                   pl.BlockSpec((rows, states), lambda i: (0, 0)),
                   pl.BlockSpec((rows, states), lambda i: (0, 0))),
        out_shape=(jax.ShapeDtypeStruct((rows, width), F32),
                   jax.ShapeDtypeStruct((rows, states), F32),
                   jax.ShapeDtypeStruct((rows, states), F32)),
        compiler_params=_cparams(1),
        name="ssm_sample",
    )(*args)


def _post_kernel(x_ref, oa_ref, os_ref, g1_ref, sh2_ref, sc2_ref, g2_ref,
                 ga_ref, gs_ref, gpm_ref, gpf_ref, gff_ref, wo_ref, wu_ref, wd_ref, y_ref,
                 *, precise):
    merged = jnp.concatenate([_rms(oa_ref[...], ga_ref[...]), _rms(os_ref[...], gs_ref[...])],
                             axis=-1)
    x1 = x_ref[...] + g1_ref[...] * _rms(_dot(merged, wo_ref[...], precise), gpm_ref[...])
    h = _rms(x1, gpf_ref[...]) * (1.0 + sc2_ref[...]) + sh2_ref[...]
    up = jnp.maximum(_dot(h, wu_ref[...], precise), 0.0)
    f = _dot(up * up, wd_ref[...], precise)
    y_ref[...] = x1 + g2_ref[...] * _rms(f, gff_ref[...])


def _post_prompt(x, o_attn, o_ssm, mod, gains, w_out, w_up, w_down):
    b, t, d = x.shape
    aw = o_attn.shape[-1]
    tb = min(PROJ_ROWS, t)
    row = lambda bi, i: (bi, i, 0)
    modspec = lambda which: pl.BlockSpec((None, None, 1, d), lambda bi, i: (bi, which, 0, 0))
    const = lambda a: pl.BlockSpec(a.shape, lambda bi, i: (0,) * a.ndim,
                                   pipeline_mode=pl.Buffered(1))
    return pl.pallas_call(
        functools.partial(_post_kernel, precise=False),
        grid=(b, t // tb),
        in_specs=[pl.BlockSpec((None, tb, d), row),
                  pl.BlockSpec((None, tb, aw), row), pl.BlockSpec((None, tb, aw), row),
                  modspec(2), modspec(3), modspec(4), modspec(5)]
                 + [const(g) for g in gains] + [const(w_out), const(w_up), const(w_down)],
        out_specs=pl.BlockSpec((None, tb, d), row),
        out_shape=jax.ShapeDtypeStruct((b, t, d), F32),
        compiler_params=_cparams(2),
        name="post_prompt",
    )(x, o_attn, o_ssm, mod, mod, mod, mod, *gains, w_out, w_up, w_down)


def _post_sample(x, o_attn, o_ssm, g1, sh2, sc2, g2, gains, w_out, w_up, w_down):
    rows, d = x.shape
    ff = w_up.shape[1]
    args = (x, o_attn, o_ssm, g1, sh2, sc2, g2, *gains, w_out, w_up, w_down)
    full = lambda a: pl.BlockSpec(a.shape, lambda i: (0,) * a.ndim, pipeline_mode=pl.Buffered(1))
    return pl.pallas_call(
        functools.partial(_post_kernel, precise=False),
        grid=(1,),
        in_specs=[full(a) for a in args],
        out_specs=pl.BlockSpec((rows, d), lambda i: (0, 0)),
        out_shape=jax.ShapeDtypeStruct((rows, d), F32),
        compiler_params=_cparams(1),
        name="post_sample",
    )(*args)


def _block_diag(blocks):
    g, r, c = blocks.shape
    eye = jnp.eye(g, dtype=blocks.dtype)
    return (blocks[:, :, None, :] * eye[:, None, :, None]).reshape(g * r, g * c)


def _layer(xp, xs, c_prompt, c_sample, pool_k, pool_v, h0_re, h0_im, page_table, p):
    b, t, d = xp.shape
    db = xs.shape[0]
    g, pstate = p["ssm_lam_re"].shape
    n_heads = p["sb_bias"].shape[0]
    attn_w = n_heads * HEAD_DIM
    ssm_w = g * SSM_GROUP
    states = g * pstate

    bbre, bbim, tab = _prep(p["ssm_lam_re"], p["ssm_lam_im"], p["ssm_log_dt"],
                            p["ssm_b_re"], p["ssm_b_im"])
    wre = _block_diag(bbre)
    wim = _block_diag(bbim)
    cre = _block_diag(p["ssm_c_re"].transpose(0, 2, 1))
    cim = _block_diag(p["ssm_c_im"].transpose(0, 2, 1))
    n_units = g // SSM_UNIT
    uw, sw = SSM_UNIT * SSM_GROUP, SSM_UNIT * pstate
    w_units = jnp.stack([jnp.concatenate([wre[m * uw:(m + 1) * uw, m * sw:(m + 1) * sw],
                                          wim[m * uw:(m + 1) * uw, m * sw:(m + 1) * sw]], axis=1)
                         for m in range(n_units)]).astype(BF16)
    c_units = jnp.stack([jnp.concatenate([cre[m * sw:(m + 1) * sw, m * uw:(m + 1) * uw],
                                          -cim[m * sw:(m + 1) * sw, m * uw:(m + 1) * uw]], axis=0)
                         for m in range(n_units)]).astype(BF16)
    d_flat = p["ssm_d"].reshape(1, ssm_w)
    b_glu = p["b_glu"].reshape(1, ssm_w)
    bias2 = p["sb_bias"] * LOG2E
    q_scale = HEAD_DIM ** -0.5 * LOG2E
    row1 = lambda a: a.reshape(1, -1)
    gains = [row1(p[n]) for n in ("g_attn_out", "g_ssm_out", "g_post_mix", "g_pre_ffn",
                                  "g_post_ffn")]

    n_mod = b + db
    pad = (-n_mod) % SUBLANES
    c_all = jnp.concatenate([c_prompt, c_sample, jnp.zeros((pad, d), F32)], axis=0)
    mod = _ada(c_all, p["w_ada"], p["b_ada"])
    mod_p = mod[:b].reshape(b, 6, 1, d)
    mod_s = mod[b:b + db].reshape(db, 6, d)

    w_in_bf = p["w_in"].astype(BF16)
    q2, k, v, u, kb, vb = _proj_prompt(xp, mod_p, row1(p["g_pre_mix"]), w_in_bf, q_scale)
    ot = _attn_prompt(bias2, q2.transpose(0, 2, 1), kb, vb)
    o_attn = ot.transpose(0, 2, 1)
    o_ssm, hre_p, him_p = _ssm_prompt(u, w_units, c_units, tab, d_flat,
                                      p["w_glu"].astype(BF16), b_glu)
    w_post = [p[n].astype(BF16) for n in ("w_out", "w_up", "w_down")]
    yp = _post_prompt(xp, o_attn, o_ssm, mod_p, gains, *w_post)

    xs2 = xs.reshape(db, d)
    qs, ks, vs, us = _proj_sample(xs2, mod_s[:, 0], mod_s[:, 1], row1(p["g_pre_mix"]),
                                  p["w_in"], q_scale)
    n_pool, page = pool_k.shape[:2]
    by_pos = lambda pool: pool.transpose(0, 2, 3, 1).reshape(n_pool, attn_w, page)
    oa_s = _attn_sample(page_table, qs, ks, vs, bias2, by_pos(pool_k), by_pos(pool_v))
    lam2 = jnp.concatenate([tab[6, 0:1], tab[7, 0:1]], axis=0)
    os_s, hre_s, him_s = _ssm_sample(us, h0_re.reshape(db, states), h0_im.reshape(db, states),
                                     lam2, wre, wim, cre, cim, d_flat, p["w_glu"], b_glu)
    ys = _post_sample(xs2, oa_s, os_s, mod_s[:, 2], mod_s[:, 3], mod_s[:, 4], mod_s[:, 5],
                      gains, *w_post)

    heads = (n_heads, HEAD_DIM)
    return (yp, ys.reshape(db, 1, d),
            k.reshape(b, t, *heads), v.reshape(b, t, *heads),
            hre_p.reshape(b, g, pstate), him_p.reshape(b, g, pstate),
            ks.reshape(db, 1, *heads), vs.reshape(db, 1, *heads),
            hre_s.reshape(db, g, pstate), him_s.reshape(db, g, pstate))


def kernel(x_prompt, x_sample, c_prompt, c_sample, cache_k, cache_v, state_ssm_re, state_ssm_im, page_table, w_ada, b_ada, g_pre_mix, w_in, sb_bias, ssm_lam_re, ssm_lam_im, ssm_log_dt, ssm_b_re, ssm_b_im, ssm_c_re, ssm_c_im, ssm_d, w_glu, b_glu, g_attn_out, g_ssm_out, w_out, g_post_mix, g_pre_ffn, w_up, w_down, g_post_ffn):
    weights = dict(w_ada=w_ada, b_ada=b_ada, g_pre_mix=g_pre_mix, w_in=w_in, sb_bias=sb_bias,
                   ssm_lam_re=ssm_lam_re, ssm_lam_im=ssm_lam_im, ssm_log_dt=ssm_log_dt,
                   ssm_b_re=ssm_b_re, ssm_b_im=ssm_b_im, ssm_c_re=ssm_c_re, ssm_c_im=ssm_c_im,
                   ssm_d=ssm_d, w_glu=w_glu, b_glu=b_glu, g_attn_out=g_attn_out,
                   g_ssm_out=g_ssm_out, w_out=w_out, g_post_mix=g_post_mix,
                   g_pre_ffn=g_pre_ffn, w_up=w_up, w_down=w_down, g_post_ffn=g_post_ffn)
    depth = w_in.shape[0]
    assert x_sample.shape[1] == 1, "decode path handles one new token per sequence"
    xp, xs = x_prompt, x_sample
    outs = []
    for l in range(depth):
        p = {n: a[l] for n, a in weights.items()}
        res = _layer(xp, xs, c_prompt, c_sample, cache_k[l], cache_v[l],
                     state_ssm_re[l], state_ssm_im[l], page_table, p)
        xp, xs = res[0], res[1]
        outs.append(res[2:])
    stacked = tuple(jnp.stack([o[i] for o in outs]) for i in range(8))
    return (xp, xs) + stacked
```

```python
import functools
import math

import numpy as np
import jax
import jax.numpy as jnp
from jax import lax
from jax.experimental import pallas as pl
from jax.experimental.pallas import tpu as pltpu

F32 = jnp.float32
BF16 = jnp.bfloat16
HIGHEST = lax.Precision.HIGHEST

RMS_EPS = 1e-6
HEAD_DIM = 64
SSM_GROUP = 16
SSM_STATE = 64
LOG2E = 1.4426950408889634
LN2 = 0.6931471805599453

LANES = 128
SUBLANES = 8
KEY_BLOCK = 128
Q_WIDTH = 512
PROJ_ROWS = 512
SSM_ROWS = 256
PAGES_PER_STEP = 16
SSM_UNIT = 16
VMEM_LIMIT = 56 * 1024 * 1024


def _cparams(n_axes):
    return pltpu.CompilerParams(dimension_semantics=("arbitrary",) * n_axes,
                                vmem_limit_bytes=VMEM_LIMIT)


def _rms(x, g):
    inv = lax.rsqrt(jnp.mean(x * x, axis=-1, keepdims=True) + RMS_EPS)
    return (x * inv) * g


def _dot(a, b, precise):
    if precise:
        return jnp.dot(a, b, precision=HIGHEST, preferred_element_type=F32)
    return jnp.dot(a.astype(BF16), b.astype(BF16), preferred_element_type=F32)


def _softplus2(z2):
    neg_abs = lax.bitcast_convert_type(
        lax.bitcast_convert_type(z2, jnp.uint32) | jnp.uint32(0x80000000), F32)
    e = jnp.exp2(neg_abs)
    return jnp.maximum(z2, 0.0) + jnp.log(1.0 + e) * LOG2E


def _split_bf16(p):
    hi = lax.bitcast_convert_type(
        lax.bitcast_convert_type(p, jnp.uint32) & jnp.uint32(0xFFFF0000), F32)
    return hi.astype(BF16), (p - hi).astype(BF16)


def _prep_kernel(lre_ref, lim_ref, dt_ref, bre_ref, bim_ref, lre8_ref, lim8_ref, dt8_ref,
                 bbre_ref, bbim_ref, tab_ref):
    def lam_bar(lre, lim, log_dt):
        dt = jnp.exp(log_dt)
        mag = jnp.exp(lre * dt)
        ang = lim * dt
        return mag * jnp.cos(ang), mag * jnp.sin(ang)

    lre, lim = lre_ref[...], lim_ref[...]
    lbr, lbi = lam_bar(lre, lim, dt_ref[...])
    nr, ni = lbr - 1.0, lbi
    den = lre * lre + lim * lim
    cre = (nr * lre + ni * lim) / den
    cim = (ni * lre - nr * lim) / den
    bre, bim = bre_ref[...], bim_ref[...]
    bbre_ref[...] = cre * bre - cim * bim
    bbim_ref[...] = cre * bim + cim * bre

    p1r, p1i = lam_bar(lre8_ref[...], lim8_ref[...], dt8_ref[...])
    row = lax.broadcasted_iota(jnp.int32, p1r.shape, 0)

    def cmul(ar, ai, br, bi):
        return ar * br - ai * bi, ar * bi + ai * br

    p2r, p2i = cmul(p1r, p1i, p1r, p1i)
    p4r, p4i = cmul(p2r, p2i, p2r, p2i)
    p8r, p8i = cmul(p4r, p4i, p4r, p4i)
    e = row + 1
    accr, acci = jnp.ones_like(p1r), jnp.zeros_like(p1r)
    for bit, (pr, pi) in enumerate(((p1r, p1i), (p2r, p2i), (p4r, p4i), (p8r, p8i))):
        on = ((e >> bit) & 1) == 1
        nr_, ni_ = cmul(accr, acci, pr, pi)
        accr = jnp.where(on, nr_, accr)
        acci = jnp.where(on, ni_, acci)
    zero = jnp.zeros_like(p1r)
    tab_ref[0] = jnp.where(row >= 1, p1r, zero)
    tab_ref[1] = jnp.where(row >= 1, p1i, zero)
    tab_ref[2] = jnp.where(row >= 2, p2r, zero)
    tab_ref[3] = jnp.where(row >= 2, p2i, zero)
    tab_ref[4] = jnp.where(row >= 4, p4r, zero)
    tab_ref[5] = jnp.where(row >= 4, p4i, zero)
    tab_ref[6] = accr
    tab_ref[7] = acci


def _prep(lam_re, lam_im, log_dt, b_re, b_im):
    g, p = lam_re.shape
    c = b_re.shape[-1]
    rows = g * c

    def rep(a):
        return jnp.broadcast_to(a[:, None, :], (g, c, p)).reshape(rows, p)

    def flat8(a):
        return jnp.broadcast_to(a.reshape(1, g * p), (SUBLANES, g * p))

    dt_gp = jnp.broadcast_to(log_dt[:, None], (g, p))
    bt = lambda b: b.transpose(0, 2, 1).reshape(rows, p)
    bbre, bbim, tab = pl.pallas_call(
        _prep_kernel,
        out_shape=(jax.ShapeDtypeStruct((rows, p), F32),
                   jax.ShapeDtypeStruct((rows, p), F32),
                   jax.ShapeDtypeStruct((8, SUBLANES, g * p), F32)),
        name="prep",
    )(rep(lam_re), rep(lam_im), rep(dt_gp), bt(b_re), bt(b_im),
      flat8(lam_re), flat8(lam_im), flat8(dt_gp))
    return bbre.reshape(g, c, p), bbim.reshape(g, c, p), tab


def _ada_kernel(c_ref, w_ref, b_ref, o_ref):
    c = c_ref[...]
    s = c * jax.nn.sigmoid(c)
    o_ref[...] = _dot(s, w_ref[...], True) + b_ref[...]


def _ada(c_all, w_ada, b_ada):
    rows, d = c_all.shape
    n = w_ada.shape[1]
    bn = 1024
    return pl.pallas_call(
        _ada_kernel,
        grid=(n // bn,),
        in_specs=[pl.BlockSpec((rows, d), lambda j: (0, 0)),
                  pl.BlockSpec((d, bn), lambda j: (0, j)),
                  pl.BlockSpec((1, bn), lambda j: (0, j))],
        out_specs=pl.BlockSpec((rows, bn), lambda j: (0, j)),
        out_shape=jax.ShapeDtypeStruct((rows, n), F32),
        compiler_params=_cparams(1),
        name="ada",
    )(c_all, w_ada, b_ada.reshape(1, n))


def _proj_kernel(x_ref, sh_ref, sc_ref, g_ref, w_ref, *out_refs, attn_w, precise, q_scale):
    h = _rms(x_ref[...], g_ref[...]) * (1.0 + sc_ref[...]) + sh_ref[...]
    p = _dot(h, w_ref[...], precise)
    q, k, v, u = (p[:, i * attn_w:(i + 1) * attn_w] for i in range(4))
    if precise:
        q_ref, k_ref, v_ref, u_ref = out_refs
        q_ref[...] = q * q_scale
    else:
        q_ref, k_ref, v_ref, u_ref, kb_ref, vb_ref = out_refs
        q_ref[...] = (q * q_scale).astype(BF16)
        kb_ref[...] = k.astype(BF16)
        vb_ref[...] = v.astype(BF16)
    k_ref[...] = k
    v_ref[...] = v
    u_ref[...] = u


def _proj_prompt(x, mod, g_pre, w_in_bf, q_scale):
    b, t, d = x.shape
    n = w_in_bf.shape[1]
    aw = n // 4
    tb = min(PROJ_ROWS, t)
    row = lambda bi, i: (bi, i, 0)
    modspec = lambda which: pl.BlockSpec((None, None, 1, d), lambda bi, i: (bi, which, 0, 0))
    f32o = jax.ShapeDtypeStruct((b, t, aw), F32)
    bfo = jax.ShapeDtypeStruct((b, t, aw), BF16)
    ospec = pl.BlockSpec((None, tb, aw), row)
    return pl.pallas_call(
        functools.partial(_proj_kernel, attn_w=aw, precise=False, q_scale=q_scale),
        grid=(b, t // tb),
        in_specs=[pl.BlockSpec((None, tb, d), row), modspec(0), modspec(1),
                  pl.BlockSpec((1, d), lambda bi, i: (0, 0)),
                  pl.BlockSpec((d, n), lambda bi, i: (0, 0))],
        out_specs=(ospec,) * 6,
        out_shape=(bfo, f32o, f32o, f32o, bfo, bfo),
        compiler_params=_cparams(2),
        name="proj_prompt",
    )(x, mod, mod, g_pre, w_in_bf)


def _proj_sample(x, sh, sc, g_pre, w_in, q_scale):
    rows, d = x.shape
    n = w_in.shape[1]
    aw = n // 4
    full = lambda shape: pl.BlockSpec(shape, lambda i: (0,) * len(shape))
    o = jax.ShapeDtypeStruct((rows, aw), F32)
    return pl.pallas_call(
        functools.partial(_proj_kernel, attn_w=aw, precise=True, q_scale=q_scale),
        grid=(1,),
        in_specs=[full((rows, d)), full((rows, d)), full((rows, d)), full((1, d)), full((d, n))],
        out_specs=(full((rows, aw)),) * 4,
        out_shape=(o, o, o, o),
        compiler_params=_cparams(1),
        name="proj_sample",
    )(x, sh, sc, g_pre, w_in)


BF16_TILE_ROWS = 16
NU_DIM = KEY_BLOCK + BF16_TILE_ROWS
CARRY_ROWS = 3
MASKED = -1e30
PIPE = 4
ROW_CHUNK = 32


def _suffix_matrix():
    s = np.arange(NU_DIM)[:, None] - BF16_TILE_ROWS
    j = np.arange(NU_DIM)[None, :]
    key = j < KEY_BLOCK
    m = np.where(key & ((j >= s) | (s < 0)), -1.0, 0.0)
    m = m + np.where((s >= 0) & (j >= KEY_BLOCK) & (j < KEY_BLOCK + CARRY_ROWS), 1.0, 0.0)
    return jnp.asarray(m, dtype=BF16)


def _attn_p_kernel(bias_ref, qt_ref, k_ref, vt_ref, nu_ref, o_ref,
                   qm_s, bm_s, z_s, p_s, tot_s, a_s, acc_s, carry_s):
    hp = pl.program_id(1)
    qi = pl.program_id(2)
    qw = qt_ref.shape[1]
    bpq = qw // KEY_BLOCK
    zeros = lambda ref, *idx: jnp.zeros(ref.shape[len(idx):], ref.dtype)

    @pl.when(qi == 0)
    def _():
        s_io = lax.broadcasted_iota(jnp.int32, (KEY_BLOCK, qw), 0)
        t_io = lax.broadcasted_iota(jnp.int32, (KEY_BLOCK, qw), 1)
        for hh in range(2):
            bias2 = bias_ref[2 * hp + hh]
            bm_s[hh, 0] = jnp.full((KEY_BLOCK, qw), bias2, F32)
            for rel in range(bpq):
                vis = s_io + rel * KEY_BLOCK < t_io
                bm_s[hh, rel + 1] = jnp.where(vis, bias2, MASKED)
            bm_s[hh, bpq + 1] = jnp.full((KEY_BLOCK, qw), MASKED, F32)

    q32 = qt_ref[...].astype(F32)
    row = lax.broadcasted_iota(jnp.int32, q32.shape, 0)
    for hh in range(2):
        own = (row >= hh * HEAD_DIM) & (row < (hh + 1) * HEAD_DIM)
        qm_s[hh] = jnp.where(own, q32, 0.0).astype(BF16)
        acc_s[hh] = zeros(acc_s, hh)
        carry_s[hh] = zeros(carry_s, hh)
        z_s[hh, 2] = jnp.full(z_s.shape[2:], MASKED, F32)
        z_s[hh, 3] = jnp.full(z_s.shape[2:], MASKED, F32)
        p_s[hh, 0] = zeros(p_s, hh, 0)
        tot_s[hh, 1] = zeros(tot_s, hh, 1)
        a_s[hh, 1] = zeros(a_s, hh, 1)

    last = (qi + 1) * bpq - 1
    row8 = lax.broadcasted_iota(jnp.int32, (SUBLANES, qw), 0)
    top16 = lambda x: lax.bitcast_convert_type(
        lax.bitcast_convert_type(x, jnp.uint32) & jnp.uint32(0xFFFF0000), F32)

    def stage(i, u):
        e, o = u % 2, (u + 1) % 2
        j_in = jnp.maximum(last - i, 0)
        k_in = k_ref[pl.ds(pl.multiple_of(j_in * KEY_BLOCK, KEY_BLOCK), KEY_BLOCK), :]
        m_in = jnp.where(i > last, bpq + 1, jnp.clip(j_in - qi * bpq, -1, bpq - 1) + 1)
        j_out = jnp.clip(last - (i - 3), 0, last)
        r = []
        for hh in range(2):
            carry = carry_s[hh] + tot_s[hh, o]
            carry_s[hh] = carry
            c_hi = top16(carry)
            c_mid = top16(carry - c_hi)
            c_lo = (carry - c_hi) - c_mid
            addends = jnp.where(row8 == 0, c_hi, jnp.where(row8 == 1, c_mid,
                                                           jnp.where(row8 == 2, c_lo, 0.0)))
            p_s[hh, e, KEY_BLOCK:NU_DIM, :] = jnp.concatenate(
                [addends, jnp.zeros_like(addends)], axis=0).astype(BF16)
            r.append(jnp.dot(nu_ref[...], p_s[hh, e], preferred_element_type=F32))
            tot_s[hh, e] = r[hh][0:SUBLANES, :]
        for hh in range(2):
            z_s[hh, u] = jnp.dot(k_in, qm_s[hh], preferred_element_type=F32) + bm_s[hh, m_in]
        if u % 2 == 1:
            j_prev = jnp.clip(last - (i - 4), 0, last)
            for hh in range(2):
                rows = slice(hh * HEAD_DIM, (hh + 1) * HEAD_DIM)
                vt = jnp.concatenate([vt_ref[j_out, rows, :], vt_ref[j_prev, rows, :]], axis=1)
                a2 = jnp.concatenate([a_s[hh, o], a_s[hh, e]], axis=0)
                acc_s[hh] += jnp.dot(vt, a2, preferred_element_type=F32)
        for hh in range(2):
            for c in range(KEY_BLOCK // ROW_CHUNK):
                sl = slice(c * ROW_CHUNK, (c + 1) * ROW_CHUNK)
                p_s[hh, o, sl, :] = _softplus2(z_s[hh, (u - 1) % PIPE, sl, :]).astype(BF16)
        for hh in range(2):
            for c in range(KEY_BLOCK // ROW_CHUNK):
                sl = slice(c * ROW_CHUNK, (c + 1) * ROW_CHUNK)
                log_a = (z_s[hh, (u - 2) % PIPE, sl, :]
                         + r[hh][BF16_TILE_ROWS + c * ROW_CHUNK:BF16_TILE_ROWS + (c + 1) * ROW_CHUNK, :])
                a_s[hh, e, sl, :] = jnp.exp2(log_a).astype(BF16)

    def body(n, c):
        for u in range(PIPE):
            stage(PIPE * n + u, u)
        return c

    lax.fori_loop(0, ((qi + 1) * bpq + 3 + PIPE - 1) // PIPE, body, 0)

    for hh in range(2):
        o_ref[hh * HEAD_DIM:(hh + 1) * HEAD_DIM, :] = acc_s[hh]


def _attn_prompt(bias2, qt, kb, vb):
    b, w, t = qt.shape
    qw = Q_WIDTH
    bpq = qw // KEY_BLOCK
    assert t % qw == 0 and bpq % 2 == 0
    pair = 2 * HEAD_DIM
    nkb = t // KEY_BLOCK
    vt = vb.reshape(b, nkb, KEY_BLOCK, w).transpose(0, 1, 3, 2)
    return pl.pallas_call(
        _attn_p_kernel,
        grid=(b, w // pair, t // qw),
        in_specs=[pl.BlockSpec(memory_space=pltpu.SMEM),
                  pl.BlockSpec((None, pair, qw), lambda bi, hp, qi: (bi, hp, qi)),
                  pl.BlockSpec((None, t, pair), lambda bi, hp, qi: (bi, 0, hp)),
                  pl.BlockSpec((None, nkb, pair, KEY_BLOCK), lambda bi, hp, qi: (bi, 0, hp, 0)),
                  pl.BlockSpec((NU_DIM, NU_DIM), lambda bi, hp, qi: (0, 0))],
        out_specs=pl.BlockSpec((None, pair, qw), lambda bi, hp, qi: (bi, hp, qi)),
        out_shape=jax.ShapeDtypeStruct((b, w, t), F32),
        scratch_shapes=[pltpu.VMEM((2, pair, qw), BF16),
                        pltpu.VMEM((2, bpq + 2, KEY_BLOCK, qw), F32),
                        pltpu.VMEM((2, PIPE, KEY_BLOCK, qw), F32),
                        pltpu.VMEM((2, 2, NU_DIM, qw), BF16),
                        pltpu.VMEM((2, 2, SUBLANES, qw), F32),
                        pltpu.VMEM((2, 2, KEY_BLOCK, qw), BF16),
                        pltpu.VMEM((2, HEAD_DIM, qw), F32),
                        pltpu.VMEM((2, SUBLANES, qw), F32)],
        compiler_params=_cparams(3),
        name="attn_prompt",
    )(bias2, qt, kb, vt, _suffix_matrix())


def _neg_suffix_matrix_lanes():
    j = np.arange(2 * KEY_BLOCK)[:, None] % KEY_BLOCK
    s = np.arange(2 * KEY_BLOCK)[None, :]
    return jnp.asarray(np.where((j >= s) | (s >= KEY_BLOCK), -1.0, 0.0), dtype=BF16)


def _attn_s_kernel(pt_ref, q_ref, kn_ref, vn_ref, bias_ref, nu_ref, *refs):
    pages = PAGES_PER_STEP
    k_refs, v_refs = refs[:pages], refs[pages:2 * pages]
    o_ref = refs[2 * pages]
    qx_s, acc_s, carry_s = refs[2 * pages + 1:]
    g = pl.program_id(1)
    rows, w = qx_s.shape
    lane_head = lax.broadcasted_iota(jnp.int32, (rows, w), 1) // HEAD_DIM
    own = lane_head == lax.broadcasted_iota(jnp.int32, (rows, w), 0)

    @pl.when(g == 0)
    def _():
        qx = jnp.where(own, jnp.broadcast_to(q_ref[0], (rows, w)), 0.0)
        qx_s[...] = qx.astype(BF16)
        tpos = lax.broadcasted_iota(jnp.int32, (rows, 1), 1)
        vis = tpos < tpos
        z_new = jnp.sum(qx * kn_ref[0], axis=-1, keepdims=True) + bias_ref[...]
        sp = _softplus2(z_new)
        log_1m = jnp.where(vis, -sp, 0.0)
        a_new = jnp.where(vis, jnp.exp2(z_new - sp), 0.0)
        acc_s[...] = a_new * jnp.where(own, jnp.broadcast_to(vn_ref[0], (rows, w)), 0.0)
        carry_s[...] = jnp.broadcast_to(log_1m, carry_s.shape)

    lanes_of = lambda x, r: x[:, r * KEY_BLOCK:(r + 1) * KEY_BLOCK]
    side_by_side = lambda page_refs: jnp.concatenate(
        [ref[0].astype(BF16) for ref in page_refs], axis=1)
    z = jnp.dot(qx_s[...], side_by_side(k_refs),
                preferred_element_type=F32) + bias_ref[...]
    hi, lo = _split_bf16(jnp.concatenate([_softplus2(lanes_of(z, r)) for r in range(pages)],
                                         axis=0))
    rr = jnp.dot(jnp.concatenate([hi, lo], axis=1), nu_ref[...],
                 preferred_element_type=F32)
    carry = carry_s[...]
    weights = []
    for r in range(pages):
        blk = rr[r * rows:(r + 1) * rows]
        weights.append(jnp.exp2(lanes_of(z, r) + blk[:, :KEY_BLOCK] + carry))
        carry = carry + blk[:, KEY_BLOCK:]
    carry_s[...] = carry
    a = jnp.concatenate(weights, axis=1).astype(BF16)
    acc_s[...] += lax.dot_general(a, side_by_side(v_refs), (((1,), (1,)), ((), ())),
                                  preferred_element_type=F32)

    @pl.when(g == pl.num_programs(1) - 1)
    def _():
        o_ref[0] = jnp.sum(jnp.where(own, acc_s[...], 0.0), axis=0, keepdims=True)


def _attn_sample(page_table, q2, k_new, v_new, bias2, pool_kt, pool_vt):
    db, w = q2.shape
    n_pool, _, page_rows = pool_kt.shape
    n_heads = w // HEAD_DIM
    n_pages = page_table.shape[1]
    pages = PAGES_PER_STEP
    assert n_pages % pages == 0 and page_rows == KEY_BLOCK and n_heads <= SUBLANES
    rows = BF16_TILE_ROWS
    bias_col = jnp.zeros((rows, 1), F32).at[:n_heads, 0].set(bias2)
    tok = lambda a: a.reshape(db, 1, w)
    tokspec = pl.BlockSpec((1, 1, w), lambda s, g, pt: (s, 0, 0))

    def page_spec(r):
        return pl.BlockSpec((1, w, page_rows),
                            lambda s, g, pt: (pt[s, n_pages - 1 - (g * pages + r)], 0, 0))

    grid_spec = pltpu.PrefetchScalarGridSpec(
        num_scalar_prefetch=1,
        grid=(db, n_pages // pages),
        in_specs=[tokspec, tokspec, tokspec,
                  pl.BlockSpec((rows, 1), lambda s, g, pt: (0, 0)),
                  pl.BlockSpec((2 * KEY_BLOCK, 2 * KEY_BLOCK), lambda s, g, pt: (0, 0))]
                 + [page_spec(r) for r in range(pages)] * 2,
        out_specs=tokspec,
        scratch_shapes=[pltpu.VMEM((rows, w), BF16),
                        pltpu.VMEM((rows, w), F32),
                        pltpu.VMEM((rows, KEY_BLOCK), F32)])
    out = pl.pallas_call(
        _attn_s_kernel,
        grid_spec=grid_spec,
        out_shape=jax.ShapeDtypeStruct((db, 1, w), F32),
        compiler_params=_cparams(2),
        name="attn_sample",
    )(page_table, tok(q2), tok(k_new), tok(v_new), bias_col, _neg_suffix_matrix_lanes(),
      *([pool_kt] * pages), *([pool_vt] * pages))
    return out.reshape(db, w)


def _gelu_glu(y, wg_ref, bg_ref, precise):
    g = 0.5 * y * (1.0 + jnp.tanh(math.sqrt(2.0 / math.pi) * (y + 0.044715 * (y * y * y))))
    return g * jax.nn.sigmoid(_dot(g, wg_ref[...], precise) + bg_ref[...])


def _ssm_p_kernel(u_ref, w_ref, cm_ref, tab_ref, d_ref, wg_ref, bg_ref,
                  o_ref, hre_ref, him_ref, bu_s, h_s):
    step = pl.program_id(0)
    nb, tc, width = u_ref.shape
    n_units = w_ref.shape[0]
    uw = width // n_units
    sw = w_ref.shape[2] // 2
    n_tiles = tc // SUBLANES

    @pl.when(step == 0)
    def _():
        h_s[:, :, 0:SUBLANES, :] = jnp.zeros((nb, n_units, SUBLANES, 2 * sw), F32)

    @pl.when(step > 0)
    def _():
        h_s[:, :, 0:SUBLANES, :] = h_s[:, :, tc:tc + SUBLANES, :]

    for b in range(nb):
        for m in range(n_units):
            bu_s[b, m] = _dot(u_ref[b, :, m * uw:(m + 1) * uw], w_ref[m], False)

    def tile_body(tb, carry):
        r0 = pl.multiple_of(tb * SUBLANES, SUBLANES)
        for b in range(nb):
            for m in range(n_units):
                for c in range(sw // LANES):
                    re = slice(c * LANES, (c + 1) * LANES)
                    im = slice(sw + c * LANES, sw + (c + 1) * LANES)
                    tl = slice(m * sw + c * LANES, m * sw + (c + 1) * LANES)
                    xr = bu_s[b, m, pl.ds(r0, SUBLANES), re]
                    xi = bu_s[b, m, pl.ds(r0, SUBLANES), im]
                    for k, ti in ((1, 0), (2, 2), (4, 4)):
                        ar, ai = tab_ref[ti, :, tl], tab_ref[ti + 1, :, tl]
                        sr = pltpu.roll(xr, k, 0)
                        si = pltpu.roll(xi, k, 0)
                        xr, xi = xr + (ar * sr - ai * si), xi + (ar * si + ai * sr)
                    prev_r = h_s[b, m, pl.ds(r0, SUBLANES), re]
                    prev_i = h_s[b, m, pl.ds(r0, SUBLANES), im]
                    pr = jnp.broadcast_to(prev_r[SUBLANES - 1:SUBLANES, :], (SUBLANES, LANES))
                    pi = jnp.broadcast_to(prev_i[SUBLANES - 1:SUBLANES, :], (SUBLANES, LANES))
                    lr, li = tab_ref[6, :, tl], tab_ref[7, :, tl]
                    xr, xi = xr + (lr * pr - li * pi), xi + (lr * pi + li * pr)
                    h_s[b, m, pl.ds(r0 + SUBLANES, SUBLANES), re] = xr
                    h_s[b, m, pl.ds(r0 + SUBLANES, SUBLANES), im] = xi
        return carry

    lax.fori_loop(0, n_tiles, tile_body, 0)

    for b in range(nb):
        ys = [_dot(h_s[b, m, SUBLANES:SUBLANES + tc, :], cm_ref[m], False) for m in range(n_units)]
        y = jnp.concatenate(ys, axis=1) + d_ref[...] * u_ref[b]
        o_ref[b] = _gelu_glu(y, wg_ref, bg_ref, False)
        for m in range(n_units):
            last = h_s[b, m, tc + SUBLANES - 1:tc + SUBLANES, :]
            hre_ref[b, :, m * sw:(m + 1) * sw] = last[:, :sw]
            him_ref[b, :, m * sw:(m + 1) * sw] = last[:, sw:]


def _ssm_prompt(u, w_units, c_units, tab, d_flat, w_glu_bf, b_glu):
    b, t, width = u.shape
    tc = min(SSM_ROWS, t)
    n_units, uw, sw2 = w_units.shape
    states = tab.shape[-1]
    const = lambda shape: pl.BlockSpec(shape, lambda i: (0,) * len(shape))
    return pl.pallas_call(
        _ssm_p_kernel,
        grid=(t // tc,),
        in_specs=[pl.BlockSpec((b, tc, width), lambda i: (0, i, 0)),
                  const(w_units.shape), const(c_units.shape), const(tab.shape),
                  const((1, width)), const(w_glu_bf.shape), const((1, width))],
        out_specs=(pl.BlockSpec((b, tc, width), lambda i: (0, i, 0)),
                   const((b, 1, states)), const((b, 1, states))),
        out_shape=(jax.ShapeDtypeStruct((b, t, width), F32),
                   jax.ShapeDtypeStruct((b, 1, states), F32),
                   jax.ShapeDtypeStruct((b, 1, states), F32)),
        scratch_shapes=[pltpu.VMEM((b, n_units, tc, sw2), F32),
                        pltpu.VMEM((b, n_units, tc + SUBLANES, sw2), F32)],
        compiler_params=_cparams(1),
        name="ssm_prompt",
    )(u, w_units, c_units, tab, d_flat, w_glu_bf, b_glu)


def _ssm_s_kernel(u_ref, h0r_ref, h0i_ref, lam_ref, wre_ref, wim_ref, cre_ref, cim_ref,
                  d_ref, wg_ref, bg_ref, o_ref, hre_ref, him_ref):
    u = u_ref[...]
    lr, li = lam_ref[0:1, :], lam_ref[1:2, :]
    h0r, h0i = h0r_ref[...], h0i_ref[...]
    hr = (lr * h0r - li * h0i) + _dot(u, wre_ref[...], True)
    hi = (lr * h0i + li * h0r) + _dot(u, wim_ref[...], True)
    hre_ref[...] = hr
    him_ref[...] = hi
    y = _dot(hr, cre_ref[...], True) - _dot(hi, cim_ref[...], True) + d_ref[...] * u
    o_ref[...] = _gelu_glu(y, wg_ref, bg_ref, True)


def _ssm_sample(u, h0r, h0i, lam2, wre, wim, cre, cim, d_flat, w_glu, b_glu):
    rows, width = u.shape
    states = h0r.shape[1]
    args = (u, h0r, h0i, lam2, wre, wim, cre, cim, d_flat, w_glu, b_glu)
    full = lambda a: pl.BlockSpec(a.shape, lambda i: (0,) * a.ndim)
    return pl.pallas_call(
        _ssm_s_kernel,
        grid=(1,),
        in_specs=[full(a) for a in args],
        out_specs=(pl.BlockSpec((rows, width), lambda i: (0, 0)),
                   pl.BlockSpec((rows, states), lambda i: (0, 0)),
                   pl.BlockSpec((rows, states), lambda i: (0, 0))),
        out_shape=(jax.ShapeDtypeStruct((rows, width), F32),
                   jax.ShapeDtypeStruct((rows, states), F32),
                   jax.ShapeDtypeStruct((rows, states), F32)),
        compiler_params=_cparams(1),
        name="ssm_sample",
    )(*args)


def _post_kernel(x_ref, oa_ref, os_ref, g1_ref, sh2_ref, sc2_ref, g2_ref,
                 ga_ref, gs_ref, gpm_ref, gpf_ref, gff_ref, wo_ref, wu_ref, wd_ref, y_ref,
                 *, precise):
    merged = jnp.concatenate([_rms(oa_ref[...], ga_ref[...]), _rms(os_ref[...], gs_ref[...])],
                             axis=-1)
    x1 = x_ref[...] + g1_ref[...] * _rms(_dot(merged, wo_ref[...], precise), gpm_ref[...])
    h = _rms(x1, gpf_ref[...]) * (1.0 + sc2_ref[...]) + sh2_ref[...]
    up = jnp.maximum(_dot(h, wu_ref[...], precise), 0.0)
    f = _dot(up * up, wd_ref[...], precise)
    y_ref[...] = x1 + g2_ref[...] * _rms(f, gff_ref[...])


def _post_prompt(x, o_attn, o_ssm, mod, gains, w_out, w_up, w_down):
    b, t, d = x.shape
    aw = o_attn.shape[-1]
    tb = min(PROJ_ROWS, t)
    row = lambda bi, i: (bi, i, 0)
    modspec = lambda which: pl.BlockSpec((None, None, 1, d), lambda bi, i: (bi, which, 0, 0))
    const = lambda a: pl.BlockSpec(a.shape, lambda bi, i: (0,) * a.ndim,
                                   pipeline_mode=pl.Buffered(1))
    return pl.pallas_call(
        functools.partial(_post_kernel, precise=False),
        grid=(b, t // tb),
        in_specs=[pl.BlockSpec((None, tb, d), row),
                  pl.BlockSpec((None, tb, aw), row), pl.BlockSpec((None, tb, aw), row),
                  modspec(2), modspec(3), modspec(4), modspec(5)]
                 + [const(g) for g in gains] + [const(w_out), const(w_up), const(w_down)],
        out_specs=pl.BlockSpec((None, tb, d), row),
        out_shape=jax.ShapeDtypeStruct((b, t, d), F32),
        compiler_params=_cparams(2),
        name="post_prompt",
    )(x, o_attn, o_ssm, mod, mod, mod, mod, *gains, w_out, w_up, w_down)


def _post_sample(x, o_attn, o_ssm, g1, sh2, sc2, g2, gains, w_out, w_up, w_down):
    rows, d = x.shape
    ff = w_up.shape[1]
    args = (x, o_attn, o_ssm, g1, sh2, sc2, g2, *gains, w_out, w_up, w_down)
    full = lambda a: pl.BlockSpec(a.shape, lambda i: (0,) * a.ndim, pipeline_mode=pl.Buffered(1))
    return pl.pallas_call(
        functools.partial(_post_kernel, precise=False),
        grid=(1,),
        in_specs=[full(a) for a in args],
        out_specs=pl.BlockSpec((rows, d), lambda i: (0, 0)),
        out_shape=jax.ShapeDtypeStruct((rows, d), F32),
        compiler_params=_cparams(1),
        name="post_sample",
    )(*args)


def _block_diag(blocks):
    g, r, c = blocks.shape
    eye = jnp.eye(g, dtype=blocks.dtype)
    return (blocks[:, :, None, :] * eye[:, None, :, None]).reshape(g * r, g * c)


def _layer(xp, xs, c_prompt, c_sample, pool_k, pool_v, h0_re, h0_im, page_table, p):
    b, t, d = xp.shape
    db = xs.shape[0]
    g, pstate = p["ssm_lam_re"].shape
    n_heads = p["sb_bias"].shape[0]
    attn_w = n_heads * HEAD_DIM
    ssm_w = g * SSM_GROUP
    states = g * pstate

    bbre, bbim, tab = _prep(p["ssm_lam_re"], p["ssm_lam_im"], p["ssm_log_dt"],
                            p["ssm_b_re"], p["ssm_b_im"])
    wre = _block_diag(bbre)
    wim = _block_diag(bbim)
    cre = _block_diag(p["ssm_c_re"].transpose(0, 2, 1))
    cim = _block_diag(p["ssm_c_im"].transpose(0, 2, 1))
    n_units = g // SSM_UNIT
    uw, sw = SSM_UNIT * SSM_GROUP, SSM_UNIT * pstate
    w_units = jnp.stack([jnp.concatenate([wre[m * uw:(m + 1) * uw, m * sw:(m + 1) * sw],
                                          wim[m * uw:(m + 1) * uw, m * sw:(m + 1) * sw]], axis=1)
                         for m in range(n_units)]).astype(BF16)
    c_units = jnp.stack([jnp.concatenate([cre[m * sw:(m + 1) * sw, m * uw:(m + 1) * uw],
                                          -cim[m * sw:(m + 1) * sw, m * uw:(m + 1) * uw]], axis=0)
                         for m in range(n_units)]).astype(BF16)
    d_flat = p["ssm_d"].reshape(1, ssm_w)
    b_glu = p["b_glu"].reshape(1, ssm_w)
    bias2 = p["sb_bias"] * LOG2E
    q_scale = HEAD_DIM ** -0.5 * LOG2E
    row1 = lambda a: a.reshape(1, -1)
    gains = [row1(p[n]) for n in ("g_attn_out", "g_ssm_out", "g_post_mix", "g_pre_ffn",
                                  "g_post_ffn")]

    n_mod = b + db
    pad = (-n_mod) % SUBLANES
    c_all = jnp.concatenate([c_prompt, c_sample, jnp.zeros((pad, d), F32)], axis=0)
    mod = _ada(c_all, p["w_ada"], p["b_ada"])
    mod_p = mod[:b].reshape(b, 6, 1, d)
    mod_s = mod[b:b + db].reshape(db, 6, d)

    w_in_bf = p["w_in"].astype(BF16)
    q2, k, v, u, kb, vb = _proj_prompt(xp, mod_p, row1(p["g_pre_mix"]), w_in_bf, q_scale)
    ot = _attn_prompt(bias2, q2.transpose(0, 2, 1), kb, vb)
    o_attn = ot.transpose(0, 2, 1)
    o_ssm, hre_p, him_p = _ssm_prompt(u, w_units, c_units, tab, d_flat,
                                      p["w_glu"].astype(BF16), b_glu)
    w_post = [p[n].astype(BF16) for n in ("w_out", "w_up", "w_down")]
    yp = _post_prompt(xp, o_attn, o_ssm, mod_p, gains, *w_post)

    xs2 = xs.reshape(db, d)
    qs, ks, vs, us = _proj_sample(xs2, mod_s[:, 0], mod_s[:, 1], row1(p["g_pre_mix"]),
                                  p["w_in"], q_scale)
    n_pool, page = pool_k.shape[:2]
    by_pos = lambda pool: pool.transpose(0, 2, 3, 1).reshape(n_pool, attn_w, page)
    oa_s = _attn_sample(page_table, qs, ks, vs, bias2, by_pos(pool_k), by_pos(pool_v))
    lam2 = jnp.concatenate([tab[6, 0:1], tab[7, 0:1]], axis=0)
    os_s, hre_s, him_s = _ssm_sample(us, h0_re.reshape(db, states), h0_im.reshape(db, states),
                                     lam2, wre, wim, cre, cim, d_flat, p["w_glu"], b_glu)
    ys = _post_sample(xs2, oa_s, os_s, mod_s[:, 2], mod_s[:, 3], mod_s[:, 4], mod_s[:, 5],
                      gains, *w_post)

    heads = (n_heads, HEAD_DIM)
    return (yp, ys.reshape(db, 1, d),
            k.reshape(b, t, *heads), v.reshape(b, t, *heads),
            hre_p.reshape(b, g, pstate), him_p.reshape(b, g, pstate),
            ks.reshape(db, 1, *heads), vs.reshape(db, 1, *heads),
            hre_s.reshape(db, g, pstate), him_s.reshape(db, g, pstate))


def kernel(x_prompt, x_sample, c_prompt, c_sample, cache_k, cache_v, state_ssm_re, state_ssm_im, page_table, w_ada, b_ada, g_pre_mix, w_in, sb_bias, ssm_lam_re, ssm_lam_im, ssm_log_dt, ssm_b_re, ssm_b_im, ssm_c_re, ssm_c_im, ssm_d, w_glu, b_glu, g_attn_out, g_ssm_out, w_out, g_post_mix, g_pre_ffn, w_up, w_down, g_post_ffn):
    weights = dict(w_ada=w_ada, b_ada=b_ada, g_pre_mix=g_pre_mix, w_in=w_in, sb_bias=sb_bias,
                   ssm_lam_re=ssm_lam_re, ssm_lam_im=ssm_lam_im, ssm_log_dt=ssm_log_dt,
                   ssm_b_re=ssm_b_re, ssm_b_im=ssm_b_im, ssm_c_re=ssm_c_re, ssm_c_im=ssm_c_im,
                   ssm_d=ssm_d, w_glu=w_glu, b_glu=b_glu, g_attn_out=g_attn_out,
                   g_ssm_out=g_ssm_out, w_out=w_out, g_post_mix=g_post_mix,
                   g_pre_ffn=g_pre_ffn, w_up=w_up, w_down=w_down, g_post_ffn=g_post_ffn)
    depth = w_in.shape[0]
    assert x_sample.shape[1] == 1, "decode path handles one new token per sequence"
    xp, xs = x_prompt, x_sample
    outs = []
    for l in range(depth):
        p = {n: a[l] for n, a in weights.items()}
        res = _layer(xp, xs, c_prompt, c_sample, cache_k[l], cache_v[l],
                     state_ssm_re[l], state_ssm_im[l], page_table, p)
        xp, xs = res[0], res[1]
        outs.append(res[2:])
    stacked = tuple(jnp.stack([o[i] for o in outs]) for i in range(8))
    return (xp, xs) + stacked
```

```python
import functools
import math

import numpy as np
import jax
import jax.numpy as jnp
from jax import lax
from jax.experimental import pallas as pl
from jax.experimental.pallas import tpu as pltpu

F32 = jnp.float32
BF16 = jnp.bfloat16
HIGHEST = lax.Precision.HIGHEST

RMS_EPS = 1e-6
HEAD_DIM = 64
SSM_GROUP = 16
SSM_STATE = 64
LOG2E = 1.4426950408889634
LN2 = 0.6931471805599453

LANES = 128
SUBLANES = 8
KEY_BLOCK = 128
Q_WIDTH = 512
PROJ_ROWS = 512
SSM_ROWS = 256
PAGES_PER_STEP = 16
SSM_UNIT = 16
VMEM_LIMIT = 56 * 1024 * 1024


def _cparams(n_axes):
    return pltpu.CompilerParams(dimension_semantics=("arbitrary",) * n_axes,
                                vmem_limit_bytes=VMEM_LIMIT)


def _rms(x, g):
    inv = lax.rsqrt(jnp.mean(x * x, axis=-1, keepdims=True) + RMS_EPS)
    return (x * inv) * g


def _dot(a, b, precise):
    if precise:
        return jnp.dot(a, b, precision=HIGHEST, preferred_element_type=F32)
    return jnp.dot(a.astype(BF16), b.astype(BF16), preferred_element_type=F32)


def _softplus2(z2):
    neg_abs = lax.bitcast_convert_type(
        lax.bitcast_convert_type(z2, jnp.uint32) | jnp.uint32(0x80000000), F32)
    e = jnp.exp2(neg_abs)
    return jnp.maximum(z2, 0.0) + jnp.log(1.0 + e) * LOG2E


def _split_bf16(p):
    hi = lax.bitcast_convert_type(
        lax.bitcast_convert_type(p, jnp.uint32) & jnp.uint32(0xFFFF0000), F32)
    return hi.astype(BF16), (p - hi).astype(BF16)


def _prep_kernel(lre_ref, lim_ref, dt_ref, bre_ref, bim_ref, lre8_ref, lim8_ref, dt8_ref,
                 bbre_ref, bbim_ref, tab_ref):
    def lam_bar(lre, lim, log_dt):
        dt = jnp.exp(log_dt)
        mag = jnp.exp(lre * dt)
        ang = lim * dt
        return mag * jnp.cos(ang), mag * jnp.sin(ang)

    lre, lim = lre_ref[...], lim_ref[...]
    lbr, lbi = lam_bar(lre, lim, dt_ref[...])
    nr, ni = lbr - 1.0, lbi
    den = lre * lre + lim * lim
    cre = (nr * lre + ni * lim) / den
    cim = (ni * lre - nr * lim) / den
    bre, bim = bre_ref[...], bim_ref[...]
    bbre_ref[...] = cre * bre - cim * bim
    bbim_ref[...] = cre * bim + cim * bre

    p1r, p1i = lam_bar(lre8_ref[...], lim8_ref[...], dt8_ref[...])
    row = lax.broadcasted_iota(jnp.int32, p1r.shape, 0)

    def cmul(ar, ai, br, bi):
        return ar * br - ai * bi, ar * bi + ai * br

    p2r, p2i = cmul(p1r, p1i, p1r, p1i)
    p4r, p4i = cmul(p2r, p2i, p2r, p2i)
    p8r, p8i = cmul(p4r, p4i, p4r, p4i)
    e = row + 1
    accr, acci = jnp.ones_like(p1r), jnp.zeros_like(p1r)
    for bit, (pr, pi) in enumerate(((p1r, p1i), (p2r, p2i), (p4r, p4i), (p8r, p8i))):
        on = ((e >> bit) & 1) == 1
        nr_, ni_ = cmul(accr, acci, pr, pi)
        accr = jnp.where(on, nr_, accr)
        acci = jnp.where(on, ni_, acci)
    zero = jnp.zeros_like(p1r)
    tab_ref[0] = jnp.where(row >= 1, p1r, zero)
    tab_ref[1] = jnp.where(row >= 1, p1i, zero)
    tab_ref[2] = jnp.where(row >= 2, p2r, zero)
    tab_ref[3] = jnp.where(row >= 2, p2i, zero)
    tab_ref[4] = jnp.where(row >= 4, p4r, zero)
    tab_ref[5] = jnp.where(row >= 4, p4i, zero)
    tab_ref[6] = accr
    tab_ref[7] = acci


def _prep(lam_re, lam_im, log_dt, b_re, b_im):
    g, p = lam_re.shape
    c = b_re.shape[-1]
    rows = g * c

    def rep(a):
        return jnp.broadcast_to(a[:, None, :], (g, c, p)).reshape(rows, p)

    def flat8(a):
        return jnp.broadcast_to(a.reshape(1, g * p), (SUBLANES, g * p))

    dt_gp = jnp.broadcast_to(log_dt[:, None], (g, p))
    bt = lambda b: b.transpose(0, 2, 1).reshape(rows, p)
    bbre, bbim, tab = pl.pallas_call(
        _prep_kernel,
        out_shape=(jax.ShapeDtypeStruct((rows, p), F32),
                   jax.ShapeDtypeStruct((rows, p), F32),
                   jax.ShapeDtypeStruct((8, SUBLANES, g * p), F32)),
        name="prep",
    )(rep(lam_re), rep(lam_im), rep(dt_gp), bt(b_re), bt(b_im),
      flat8(lam_re), flat8(lam_im), flat8(dt_gp))
    return bbre.reshape(g, c, p), bbim.reshape(g, c, p), tab


def _ada_kernel(c_ref, w_ref, b_ref, o_ref):
    c = c_ref[...]
    s = c * jax.nn.sigmoid(c)
    o_ref[...] = _dot(s, w_ref[...], True) + b_ref[...]


def _ada(c_all, w_ada, b_ada):
    rows, d = c_all.shape
    n = w_ada.shape[1]
    bn = 1024
    return pl.pallas_call(
        _ada_kernel,
        grid=(n // bn,),
        in_specs=[pl.BlockSpec((rows, d), lambda j: (0, 0)),
                  pl.BlockSpec((d, bn), lambda j: (0, j)),
                  pl.BlockSpec((1, bn), lambda j: (0, j))],
        out_specs=pl.BlockSpec((rows, bn), lambda j: (0, j)),
        out_shape=jax.ShapeDtypeStruct((rows, n), F32),
        compiler_params=_cparams(1),
        name="ada",
    )(c_all, w_ada, b_ada.reshape(1, n))


def _proj_kernel(x_ref, sh_ref, sc_ref, g_ref, w_ref, *out_refs, attn_w, precise, q_scale):
    h = _rms(x_ref[...], g_ref[...]) * (1.0 + sc_ref[...]) + sh_ref[...]
    p = _dot(h, w_ref[...], precise)
    q, k, v, u = (p[:, i * attn_w:(i + 1) * attn_w] for i in range(4))
    if precise:
        q_ref, k_ref, v_ref, u_ref = out_refs
        q_ref[...] = q * q_scale
    else:
        q_ref, k_ref, v_ref, u_ref, kb_ref, vb_ref = out_refs
        q_ref[...] = (q * q_scale).astype(BF16)
        kb_ref[...] = k.astype(BF16)
        vb_ref[...] = v.astype(BF16)
    k_ref[...] = k
    v_ref[...] = v
    u_ref[...] = u


def _proj_prompt(x, mod, g_pre, w_in_bf, q_scale):
    b, t, d = x.shape
    n = w_in_bf.shape[1]
    aw = n // 4
    tb = min(PROJ_ROWS, t)
    row = lambda bi, i: (bi, i, 0)
    modspec = lambda which: pl.BlockSpec((None, None, 1, d), lambda bi, i: (bi, which, 0, 0))
    f32o = jax.ShapeDtypeStruct((b, t, aw), F32)
    bfo = jax.ShapeDtypeStruct((b, t, aw), BF16)
    ospec = pl.BlockSpec((None, tb, aw), row)
    return pl.pallas_call(
        functools.partial(_proj_kernel, attn_w=aw, precise=False, q_scale=q_scale),
        grid=(b, t // tb),
        in_specs=[pl.BlockSpec((None, tb, d), row), modspec(0), modspec(1),
                  pl.BlockSpec((1, d), lambda bi, i: (0, 0)),
                  pl.BlockSpec((d, n), lambda bi, i: (0, 0))],
        out_specs=(ospec,) * 6,
        out_shape=(bfo, f32o, f32o, f32o, bfo, bfo),
        compiler_params=_cparams(2),
        name="proj_prompt",
    )(x, mod, mod, g_pre, w_in_bf)


def _proj_sample(x, sh, sc, g_pre, w_in, q_scale):
    rows, d = x.shape
    n = w_in.shape[1]
    aw = n // 4
    full = lambda shape: pl.BlockSpec(shape, lambda i: (0,) * len(shape))
    o = jax.ShapeDtypeStruct((rows, aw), F32)
    return pl.pallas_call(
        functools.partial(_proj_kernel, attn_w=aw, precise=True, q_scale=q_scale),
        grid=(1,),
        in_specs=[full((rows, d)), full((rows, d)), full((rows, d)), full((1, d)), full((d, n))],
        out_specs=(full((rows, aw)),) * 4,
        out_shape=(o, o, o, o),
        compiler_params=_cparams(1),
        name="proj_sample",
    )(x, sh, sc, g_pre, w_in)


BF16_TILE_ROWS = 16
NU_DIM = KEY_BLOCK + BF16_TILE_ROWS
CARRY_ROWS = 3
MASKED = -1e30
PIPE = 4
ROW_CHUNK = 32


def _suffix_matrix():
    s = np.arange(NU_DIM)[:, None] - BF16_TILE_ROWS
    j = np.arange(NU_DIM)[None, :]
    key = j < KEY_BLOCK
    m = np.where(key & ((j >= s) | (s < 0)), -1.0, 0.0)
    m = m + np.where((s >= 0) & (j >= KEY_BLOCK) & (j < KEY_BLOCK + CARRY_ROWS), 1.0, 0.0)
    return jnp.asarray(m, dtype=BF16)


def _attn_p_kernel(bias_ref, qt_ref, k_ref, vt_ref, nu_ref, o_ref,
                   qm_s, bm_s, z_s, p_s, tot_s, a_s, acc_s, carry_s):
    hp = pl.program_id(1)
    qi = pl.program_id(2)
    qw = qt_ref.shape[1]
    bpq = qw // KEY_BLOCK
    zeros = lambda ref, *idx: jnp.zeros(ref.shape[len(idx):], ref.dtype)

    @pl.when(qi == 0)
    def _():
        s_io = lax.broadcasted_iota(jnp.int32, (KEY_BLOCK, qw), 0)
        t_io = lax.broadcasted_iota(jnp.int32, (KEY_BLOCK, qw), 1)
        for hh in range(2):
            bias2 = bias_ref[2 * hp + hh]
            bm_s[hh, 0] = jnp.full((KEY_BLOCK, qw), bias2, F32)
            for rel in range(bpq):
                vis = s_io + rel * KEY_BLOCK < t_io
                bm_s[hh, rel + 1] = jnp.where(vis, bias2, MASKED)
            bm_s[hh, bpq + 1] = jnp.full((KEY_BLOCK, qw), MASKED, F32)

    q32 = qt_ref[...].astype(F32)
    row = lax.broadcasted_iota(jnp.int32, q32.shape, 0)
    for hh in range(2):
        own = (row >= hh * HEAD_DIM) & (row < (hh + 1) * HEAD_DIM)
        qm_s[hh] = jnp.where(own, q32, 0.0).astype(BF16)
        acc_s[hh] = zeros(acc_s, hh)
        carry_s[hh] = zeros(carry_s, hh)
        z_s[hh, 2] = jnp.full(z_s.shape[2:], MASKED, F32)
        z_s[hh, 3] = jnp.full(z_s.shape[2:], MASKED, F32)
        p_s[hh, 0] = zeros(p_s, hh, 0)
        tot_s[hh, 1] = zeros(tot_s, hh, 1)
        a_s[hh, 1] = zeros(a_s, hh, 1)

    last = (qi + 1) * bpq - 1
    row8 = lax.broadcasted_iota(jnp.int32, (SUBLANES, qw), 0)
    top16 = lambda x: lax.bitcast_convert_type(
        lax.bitcast_convert_type(x, jnp.uint32) & jnp.uint32(0xFFFF0000), F32)

    def stage(i, u):
        e, o = u % 2, (u + 1) % 2
        j_in = jnp.maximum(last - i, 0)
        k_in = k_ref[pl.ds(pl.multiple_of(j_in * KEY_BLOCK, KEY_BLOCK), KEY_BLOCK), :]
        m_in = jnp.where(i > last, bpq + 1, jnp.clip(j_in - qi * bpq, -1, bpq - 1) + 1)
        j_out = jnp.clip(last - (i - 3), 0, last)
        r = []
        for hh in range(2):
            carry = carry_s[hh] + tot_s[hh, o]
            carry_s[hh] = carry
            c_hi = top16(carry)
            c_mid = top16(carry - c_hi)
            c_lo = (carry - c_hi) - c_mid
            addends = jnp.where(row8 == 0, c_hi, jnp.where(row8 == 1, c_mid,
                                                           jnp.where(row8 == 2, c_lo, 0.0)))
            p_s[hh, e, KEY_BLOCK:NU_DIM, :] = jnp.concatenate(
                [addends, jnp.zeros_like(addends)], axis=0).astype(BF16)
            r.append(jnp.dot(nu_ref[...], p_s[hh, e], preferred_element_type=F32))
            tot_s[hh, e] = r[hh][0:SUBLANES, :]
        if u % 2 == 1:
            j_prev = jnp.clip(last - (i - 4), 0, last)
            for hh in range(2):
                rows = slice(hh * HEAD_DIM, (hh + 1) * HEAD_DIM)
                vt = jnp.concatenate([vt_ref[j_out, rows, :], vt_ref[j_prev, rows, :]], axis=1)
                a2 = jnp.concatenate([a_s[hh, o], a_s[hh, e]], axis=0)
                acc_s[hh] += jnp.dot(vt, a2, preferred_element_type=F32)
        for hh in range(2):
            z_s[hh, u] = jnp.dot(k_in, qm_s[hh], preferred_element_type=F32) + bm_s[hh, m_in]
        for hh in range(2):
            for c in range(KEY_BLOCK // ROW_CHUNK):
                sl = slice(c * ROW_CHUNK, (c + 1) * ROW_CHUNK)
                p_s[hh, o, sl, :] = _softplus2(z_s[hh, (u - 1) % PIPE, sl, :]).astype(BF16)
        for hh in range(2):
            for c in range(KEY_BLOCK // ROW_CHUNK):
                sl = slice(c * ROW_CHUNK, (c + 1) * ROW_CHUNK)
                log_a = (z_s[hh, (u - 2) % PIPE, sl, :]
                         + r[hh][BF16_TILE_ROWS + c * ROW_CHUNK:BF16_TILE_ROWS + (c + 1) * ROW_CHUNK, :])
                a_s[hh, e, sl, :] = jnp.exp2(log_a).astype(BF16)

    def body(n, c):
        for u in range(PIPE):
            stage(PIPE * n + u, u)
        return c

    lax.fori_loop(0, ((qi + 1) * bpq + 3 + PIPE - 1) // PIPE, body, 0)

    for hh in range(2):
        o_ref[hh * HEAD_DIM:(hh + 1) * HEAD_DIM, :] = acc_s[hh]


def _attn_prompt(bias2, qt, kb, vb):
    b, w, t = qt.shape
    qw = Q_WIDTH
    bpq = qw // KEY_BLOCK
    assert t % qw == 0 and bpq % 2 == 0
    pair = 2 * HEAD_DIM
    nkb = t // KEY_BLOCK
    vt = vb.reshape(b, nkb, KEY_BLOCK, w).transpose(0, 1, 3, 2)
    return pl.pallas_call(
        _attn_p_kernel,
        grid=(b, w // pair, t // qw),
        in_specs=[pl.BlockSpec(memory_space=pltpu.SMEM),
                  pl.BlockSpec((None, pair, qw), lambda bi, hp, qi: (bi, hp, qi)),
                  pl.BlockSpec((None, t, pair), lambda bi, hp, qi: (bi, 0, hp)),
                  pl.BlockSpec((None, nkb, pair, KEY_BLOCK), lambda bi, hp, qi: (bi, 0, hp, 0)),
                  pl.BlockSpec((NU_DIM, NU_DIM), lambda bi, hp, qi: (0, 0))],
        out_specs=pl.BlockSpec((None, pair, qw), lambda bi, hp, qi: (bi, hp, qi)),
        out_shape=jax.ShapeDtypeStruct((b, w, t), F32),
        scratch_shapes=[pltpu.VMEM((2, pair, qw), BF16),
                        pltpu.VMEM((2, bpq + 2, KEY_BLOCK, qw), F32),
                        pltpu.VMEM((2, PIPE, KEY_BLOCK, qw), F32),
                        pltpu.VMEM((2, 2, NU_DIM, qw), BF16),
                        pltpu.VMEM((2, 2, SUBLANES, qw), F32),
                        pltpu.VMEM((2, 2, KEY_BLOCK, qw), BF16),
                        pltpu.VMEM((2, HEAD_DIM, qw), F32),
                        pltpu.VMEM((2, SUBLANES, qw), F32)],
        compiler_params=_cparams(3),
        name="attn_prompt",
    )(bias2, qt, kb, vt, _suffix_matrix())


def _neg_suffix_matrix_lanes():
    j = np.arange(2 * KEY_BLOCK)[:, None] % KEY_BLOCK
    s = np.arange(2 * KEY_BLOCK)[None, :]
    return jnp.asarray(np.where((j >= s) | (s >= KEY_BLOCK), -1.0, 0.0), dtype=BF16)


def _attn_s_kernel(pt_ref, q_ref, kn_ref, vn_ref, bias_ref, nu_ref, *refs):
    pages = PAGES_PER_STEP
    k_refs, v_refs = refs[:pages], refs[pages:2 * pages]
    o_ref = refs[2 * pages]
    qx_s, acc_s, carry_s = refs[2 * pages + 1:]
    g = pl.program_id(1)
    rows, w = qx_s.shape
    lane_head = lax.broadcasted_iota(jnp.int32, (rows, w), 1) // HEAD_DIM
    own = lane_head == lax.broadcasted_iota(jnp.int32, (rows, w), 0)

    @pl.when(g == 0)
    def _():
        qx = jnp.where(own, jnp.broadcast_to(q_ref[0], (rows, w)), 0.0)
        qx_s[...] = qx.astype(BF16)
        tpos = lax.broadcasted_iota(jnp.int32, (rows, 1), 1)
        vis = tpos < tpos
        z_new = jnp.sum(qx * kn_ref[0], axis=-1, keepdims=True) + bias_ref[...]
        sp = _softplus2(z_new)
        log_1m = jnp.where(vis, -sp, 0.0)
        a_new = jnp.where(vis, jnp.exp2(z_new - sp), 0.0)
        acc_s[...] = a_new * jnp.where(own, jnp.broadcast_to(vn_ref[0], (rows, w)), 0.0)
        carry_s[...] = jnp.broadcast_to(log_1m, carry_s.shape)

    lanes_of = lambda x, r: x[:, r * KEY_BLOCK:(r + 1) * KEY_BLOCK]
    side_by_side = lambda page_refs: jnp.concatenate(
        [ref[0].astype(BF16) for ref in page_refs], axis=1)
    z = jnp.dot(qx_s[...], side_by_side(k_refs),
                preferred_element_type=F32) + bias_ref[...]
    hi, lo = _split_bf16(jnp.concatenate([_softplus2(lanes_of(z, r)) for r in range(pages)],
                                         axis=0))
    rr = jnp.dot(jnp.concatenate([hi, lo], axis=1), nu_ref[...],
                 preferred_element_type=F32)
    carry = carry_s[...]
    weights = []
    for r in range(pages):
        blk = rr[r * rows:(r + 1) * rows]
        weights.append(jnp.exp2(lanes_of(z, r) + blk[:, :KEY_BLOCK] + carry))
        carry = carry + blk[:, KEY_BLOCK:]
    carry_s[...] = carry
    a = jnp.concatenate(weights, axis=1).astype(BF16)
    acc_s[...] += lax.dot_general(a, side_by_side(v_refs), (((1,), (1,)), ((), ())),
                                  preferred_element_type=F32)

    @pl.when(g == pl.num_programs(1) - 1)
    def _():
        o_ref[0] = jnp.sum(jnp.where(own, acc_s[...], 0.0), axis=0, keepdims=True)


def _attn_sample(page_table, q2, k_new, v_new, bias2, pool_kt, pool_vt):
    db, w = q2.shape
    n_pool, _, page_rows = pool_kt.shape
    n_heads = w // HEAD_DIM
    n_pages = page_table.shape[1]
    pages = PAGES_PER_STEP
    assert n_pages % pages == 0 and page_rows == KEY_BLOCK and n_heads <= SUBLANES
    rows = BF16_TILE_ROWS
    bias_col = jnp.zeros((rows, 1), F32).at[:n_heads, 0].set(bias2)
    tok = lambda a: a.reshape(db, 1, w)
    tokspec = pl.BlockSpec((1, 1, w), lambda s, g, pt: (s, 0, 0))

    def page_spec(r):
        return pl.BlockSpec((1, w, page_rows),
                            lambda s, g, pt: (pt[s, n_pages - 1 - (g * pages + r)], 0, 0))

    grid_spec = pltpu.PrefetchScalarGridSpec(
        num_scalar_prefetch=1,
        grid=(db, n_pages // pages),
        in_specs=[tokspec, tokspec, tokspec,
                  pl.BlockSpec((rows, 1), lambda s, g, pt: (0, 0)),
                  pl.BlockSpec((2 * KEY_BLOCK, 2 * KEY_BLOCK), lambda s, g, pt: (0, 0))]
                 + [page_spec(r) for r in range(pages)] * 2,
        out_specs=tokspec,
        scratch_shapes=[pltpu.VMEM((rows, w), BF16),
                        pltpu.VMEM((rows, w), F32),
                        pltpu.VMEM((rows, KEY_BLOCK), F32)])
    out = pl.pallas_call(
        _attn_s_kernel,
        grid_spec=grid_spec,
        out_shape=jax.ShapeDtypeStruct((db, 1, w), F32),
        compiler_params=_cparams(2),
        name="attn_sample",
    )(page_table, tok(q2), tok(k_new), tok(v_new), bias_col, _neg_suffix_matrix_lanes(),
      *([pool_kt] * pages), *([pool_vt] * pages))
    return out.reshape(db, w)


def _gelu_glu(y, wg_ref, bg_ref, precise):
    g = 0.5 * y * (1.0 + jnp.tanh(math.sqrt(2.0 / math.pi) * (y + 0.044715 * (y * y * y))))
    return g * jax.nn.sigmoid(_dot(g, wg_ref[...], precise) + bg_ref[...])


def _ssm_p_kernel(u_ref, w_ref, cm_ref, tab_ref, d_ref, wg_ref, bg_ref,
                  o_ref, hre_ref, him_ref, bu_s, h_s):
    step = pl.program_id(0)
    nb, tc, width = u_ref.shape
    n_units = w_ref.shape[0]
    uw = width // n_units
    sw = w_ref.shape[2] // 2
    n_tiles = tc // SUBLANES

    @pl.when(step == 0)
    def _():
        h_s[:, :, 0:SUBLANES, :] = jnp.zeros((nb, n_units, SUBLANES, 2 * sw), F32)

    @pl.when(step > 0)
    def _():
        h_s[:, :, 0:SUBLANES, :] = h_s[:, :, tc:tc + SUBLANES, :]

    for b in range(nb):
        for m in range(n_units):
            bu_s[b, m] = _dot(u_ref[b, :, m * uw:(m + 1) * uw], w_ref[m], False)

    def tile_body(tb, carry):
        r0 = pl.multiple_of(tb * SUBLANES, SUBLANES)
        for b in range(nb):
            for m in range(n_units):
                for c in range(sw // LANES):
                    re = slice(c * LANES, (c + 1) * LANES)
                    im = slice(sw + c * LANES, sw + (c + 1) * LANES)
                    tl = slice(m * sw + c * LANES, m * sw + (c + 1) * LANES)
                    xr = bu_s[b, m, pl.ds(r0, SUBLANES), re]
                    xi = bu_s[b, m, pl.ds(r0, SUBLANES), im]
                    for k, ti in ((1, 0), (2, 2), (4, 4)):
                        ar, ai = tab_ref[ti, :, tl], tab_ref[ti + 1, :, tl]
                        sr = pltpu.roll(xr, k, 0)
                        si = pltpu.roll(xi, k, 0)
                        xr, xi = xr + (ar * sr - ai * si), xi + (ar * si + ai * sr)
                    prev_r = h_s[b, m, pl.ds(r0, SUBLANES), re]
                    prev_i = h_s[b, m, pl.ds(r0, SUBLANES), im]
                    pr = jnp.broadcast_to(prev_r[SUBLANES - 1:SUBLANES, :], (SUBLANES, LANES))
                    pi = jnp.broadcast_to(prev_i[SUBLANES - 1:SUBLANES, :], (SUBLANES, LANES))
                    lr, li = tab_ref[6, :, tl], tab_ref[7, :, tl]
                    xr, xi = xr + (lr * pr - li * pi), xi + (lr * pi + li * pr)
                    h_s[b, m, pl.ds(r0 + SUBLANES, SUBLANES), re] = xr
                    h_s[b, m, pl.ds(r0 + SUBLANES, SUBLANES), im] = xi
        return carry

    lax.fori_loop(0, n_tiles, tile_body, 0)

    for b in range(nb):
        ys = [_dot(h_s[b, m, SUBLANES:SUBLANES + tc, :], cm_ref[m], False) for m in range(n_units)]
        y = jnp.concatenate(ys, axis=1) + d_ref[...] * u_ref[b]
        o_ref[b] = _gelu_glu(y, wg_ref, bg_ref, False)
        for m in range(n_units):
            last = h_s[b, m, tc + SUBLANES - 1:tc + SUBLANES, :]
            hre_ref[b, :, m * sw:(m + 1) * sw] = last[:, :sw]
            him_ref[b, :, m * sw:(m + 1) * sw] = last[:, sw:]


def _ssm_prompt(u, w_units, c_units, tab, d_flat, w_glu_bf, b_glu):
    b, t, width = u.shape
    tc = min(SSM_ROWS, t)
    n_units, uw, sw2 = w_units.shape
    states = tab.shape[-1]
    const = lambda shape: pl.BlockSpec(shape, lambda i: (0,) * len(shape))
    return pl.pallas_call(
        _ssm_p_kernel,
        grid=(t // tc,),
        in_specs=[pl.BlockSpec((b, tc, width), lambda i: (0, i, 0)),
                  const(w_units.shape), const(c_units.shape), const(tab.shape),
                  const((1, width)), const(w_glu_bf.shape), const((1, width))],
        out_specs=(pl.BlockSpec((b, tc, width), lambda i: (0, i, 0)),
                   const((b, 1, states)), const((b, 1, states))),
        out_shape=(jax.ShapeDtypeStruct((b, t, width), F32),
                   jax.ShapeDtypeStruct((b, 1, states), F32),
                   jax.ShapeDtypeStruct((b, 1, states), F32)),
        scratch_shapes=[pltpu.VMEM((b, n_units, tc, sw2), F32),
                        pltpu.VMEM((b, n_units, tc + SUBLANES, sw2), F32)],
        compiler_params=_cparams(1),
        name="ssm_prompt",
    )(u, w_units, c_units, tab, d_flat, w_glu_bf, b_glu)


def _ssm_s_kernel(u_ref, h0r_ref, h0i_ref, lam_ref, wre_ref, wim_ref, cre_ref, cim_ref,
                  d_ref, wg_ref, bg_ref, o_ref, hre_ref, him_ref):
    u = u_ref[...]
    lr, li = lam_ref[0:1, :], lam_ref[1:2, :]
    h0r, h0i = h0r_ref[...], h0i_ref[...]
    hr = (lr * h0r - li * h0i) + _dot(u, wre_ref[...], True)
    hi = (lr * h0i + li * h0r) + _dot(u, wim_ref[...], True)
    hre_ref[...] = hr
    him_ref[...] = hi
    y = _dot(hr, cre_ref[...], True) - _dot(hi, cim_ref[...], True) + d_ref[...] * u
    o_ref[...] = _gelu_glu(y, wg_ref, bg_ref, True)


def _ssm_sample(u, h0r, h0i, lam2, wre, wim, cre, cim, d_flat, w_glu, b_glu):
    rows, width = u.shape
    states = h0r.shape[1]
    args = (u, h0r, h0i, lam2, wre, wim, cre, cim, d_flat, w_glu, b_glu)
    full = lambda a: pl.BlockSpec(a.shape, lambda i: (0,) * a.ndim)
    return pl.pallas_call(
        _ssm_s_kernel,
        grid=(1,),
        in_specs=[full(a) for a in args],
        out_specs=(pl.BlockSpec((rows, width), lambda i: (0, 0)),
                   pl.BlockSpec((rows, states), lambda i: (0, 0)),
                   pl.BlockSpec((rows, states), lambda i: (0, 0))),
        out_shape=(jax.ShapeDtypeStruct((rows, width), F32),
                   jax.ShapeDtypeStruct((rows, states), F32),
                   jax.ShapeDtypeStruct((rows, states), F32)),
        compiler_params=_cparams(1),
        name="ssm_sample",
    )(*args)


def _post_kernel(x_ref, oa_ref, os_ref, g1_ref, sh2_ref, sc2_ref, g2_ref,
                 ga_ref, gs_ref, gpm_ref, gpf_ref, gff_ref, wo_ref, wu_ref, wd_ref, y_ref,
                 *, precise):
    merged = jnp.concatenate([_rms(oa_ref[...], ga_ref[...]), _rms(os_ref[...], gs_ref[...])],
                             axis=-1)
    x1 = x_ref[...] + g1_ref[...] * _rms(_dot(merged, wo_ref[...], precise), gpm_ref[...])
    h = _rms(x1, gpf_ref[...]) * (1.0 + sc2_ref[...]) + sh2_ref[...]
    up = jnp.maximum(_dot(h, wu_ref[...], precise), 0.0)
    f = _dot(up * up, wd_ref[...], precise)
    y_ref[...] = x1 + g2_ref[...] * _rms(f, gff_ref[...])


def _post_prompt(x, o_attn, o_ssm, mod, gains, w_out, w_up, w_down):
    b, t, d = x.shape
    aw = o_attn.shape[-1]
    tb = min(PROJ_ROWS, t)
    row = lambda bi, i: (bi, i, 0)
    modspec = lambda which: pl.BlockSpec((None, None, 1, d), lambda bi, i: (bi, which, 0, 0))
    const = lambda a: pl.BlockSpec(a.shape, lambda bi, i: (0,) * a.ndim,
                                   pipeline_mode=pl.Buffered(1))
    return pl.pallas_call(
        functools.partial(_post_kernel, precise=False),
        grid=(b, t // tb),
        in_specs=[pl.BlockSpec((None, tb, d), row),
                  pl.BlockSpec((None, tb, aw), row), pl.BlockSpec((None, tb, aw), row),
                  modspec(2), modspec(3), modspec(4), modspec(5)]
                 + [const(g) for g in gains] + [const(w_out), const(w_up), const(w_down)],
        out_specs=pl.BlockSpec((None, tb, d), row),
        out_shape=jax.ShapeDtypeStruct((b, t, d), F32),
        compiler_params=_cparams(2),
        name="post_prompt",
    )(x, o_attn, o_ssm, mod, mod, mod, mod, *gains, w_out, w_up, w_down)


def _post_sample(x, o_attn, o_ssm, g1, sh2, sc2, g2, gains, w_out, w_up, w_down):
    rows, d = x.shape
    ff = w_up.shape[1]
    args = (x, o_attn, o_ssm, g1, sh2, sc2, g2, *gains, w_out, w_up, w_down)
    full = lambda a: pl.BlockSpec(a.shape, lambda i: (0,) * a.ndim, pipeline_mode=pl.Buffered(1))
    return pl.pallas_call(
        functools.partial(_post_kernel, precise=False),
        grid=(1,),
        in_specs=[full(a) for a in args],
        out_specs=pl.BlockSpec((rows, d), lambda i: (0, 0)),
        out_shape=jax.ShapeDtypeStruct((rows, d), F32),
        compiler_params=_cparams(1),
        name="post_sample",
    )(*args)


def _block_diag(blocks):
    g, r, c = blocks.shape
    eye = jnp.eye(g, dtype=blocks.dtype)
    return (blocks[:, :, None, :] * eye[:, None, :, None]).reshape(g * r, g * c)


def _layer(xp, xs, c_prompt, c_sample, pool_k, pool_v, h0_re, h0_im, page_table, p):
    b, t, d = xp.shape
    db = xs.shape[0]
    g, pstate = p["ssm_lam_re"].shape
    n_heads = p["sb_bias"].shape[0]
    attn_w = n_heads * HEAD_DIM
    ssm_w = g * SSM_GROUP
    states = g * pstate

    bbre, bbim, tab = _prep(p["ssm_lam_re"], p["ssm_lam_im"], p["ssm_log_dt"],
                            p["ssm_b_re"], p["ssm_b_im"])
    wre = _block_diag(bbre)
    wim = _block_diag(bbim)
    cre = _block_diag(p["ssm_c_re"].transpose(0, 2, 1))
    cim = _block_diag(p["ssm_c_im"].transpose(0, 2, 1))
    n_units = g // SSM_UNIT
    uw, sw = SSM_UNIT * SSM_GROUP, SSM_UNIT * pstate
    w_units = jnp.stack([jnp.concatenate([wre[m * uw:(m + 1) * uw, m * sw:(m + 1) * sw],
                                          wim[m * uw:(m + 1) * uw, m * sw:(m + 1) * sw]], axis=1)
                         for m in range(n_units)]).astype(BF16)
    c_units = jnp.stack([jnp.concatenate([cre[m * sw:(m + 1) * sw, m * uw:(m + 1) * uw],
                                          -cim[m * sw:(m + 1) * sw, m * uw:(m + 1) * uw]], axis=0)
                         for m in range(n_units)]).astype(BF16)
    d_flat = p["ssm_d"].reshape(1, ssm_w)
    b_glu = p["b_glu"].reshape(1, ssm_w)
    bias2 = p["sb_bias"] * LOG2E
    q_scale = HEAD_DIM ** -0.5 * LOG2E
    row1 = lambda a: a.reshape(1, -1)
    gains = [row1(p[n]) for n in ("g_attn_out", "g_ssm_out", "g_post_mix", "g_pre_ffn",
                                  "g_post_ffn")]

    n_mod = b + db
    pad = (-n_mod) % SUBLANES
    c_all = jnp.concatenate([c_prompt, c_sample, jnp.zeros((pad, d), F32)], axis=0)
    mod = _ada(c_all, p["w_ada"], p["b_ada"])
    mod_p = mod[:b].reshape(b, 6, 1, d)
    mod_s = mod[b:b + db].reshape(db, 6, d)

    w_in_bf = p["w_in"].astype(BF16)
    q2, k, v, u, kb, vb = _proj_prompt(xp, mod_p, row1(p["g_pre_mix"]), w_in_bf, q_scale)
    ot = _attn_prompt(bias2, q2.transpose(0, 2, 1), kb, vb)
    o_attn = ot.transpose(0, 2, 1)
    o_ssm, hre_p, him_p = _ssm_prompt(u, w_units, c_units, tab, d_flat,
                                      p["w_glu"].astype(BF16), b_glu)
    w_post = [p[n].astype(BF16) for n in ("w_out", "w_up", "w_down")]
    yp = _post_prompt(xp, o_attn, o_ssm, mod_p, gains, *w_post)

    xs2 = xs.reshape(db, d)
    qs, ks, vs, us = _proj_sample(xs2, mod_s[:, 0], mod_s[:, 1], row1(p["g_pre_mix"]),
                                  p["w_in"], q_scale)
    n_pool, page = pool_k.shape[:2]
    by_pos = lambda pool: pool.transpose(0, 2, 3, 1).reshape(n_pool, attn_w, page)
    oa_s = _attn_sample(page_table, qs, ks, vs, bias2, by_pos(pool_k), by_pos(pool_v))
    lam2 = jnp.concatenate([tab[6, 0:1], tab[7, 0:1]], axis=0)
    os_s, hre_s, him_s = _ssm_sample(us, h0_re.reshape(db, states), h0_im.reshape(db, states),
                                     lam2, wre, wim, cre, cim, d_flat, p["w_glu"], b_glu)
    ys = _post_sample(xs2, oa_s, os_s, mod_s[:, 2], mod_s[:, 3], mod_s[:, 4], mod_s[:, 5],
                      gains, *w_post)

    heads = (n_heads, HEAD_DIM)
    return (yp, ys.reshape(db, 1, d),
            k.reshape(b, t, *heads), v.reshape(b, t, *heads),
            hre_p.reshape(b, g, pstate), him_p.reshape(b, g, pstate),
            ks.reshape(db, 1, *heads), vs.reshape(db, 1, *heads),
            hre_s.reshape(db, g, pstate), him_s.reshape(db, g, pstate))


def kernel(x_prompt, x_sample, c_prompt, c_sample, cache_k, cache_v, state_ssm_re, state_ssm_im, page_table, w_ada, b_ada, g_pre_mix, w_in, sb_bias, ssm_lam_re, ssm_lam_im, ssm_log_dt, ssm_b_re, ssm_b_im, ssm_c_re, ssm_c_im, ssm_d, w_glu, b_glu, g_attn_out, g_ssm_out, w_out, g_post_mix, g_pre_ffn, w_up, w_down, g_post_ffn):
    weights = dict(w_ada=w_ada, b_ada=b_ada, g_pre_mix=g_pre_mix, w_in=w_in, sb_bias=sb_bias,
                   ssm_lam_re=ssm_lam_re, ssm_lam_im=ssm_lam_im, ssm_log_dt=ssm_log_dt,
                   ssm_b_re=ssm_b_re, ssm_b_im=ssm_b_im, ssm_c_re=ssm_c_re, ssm_c_im=ssm_c_im,
                   ssm_d=ssm_d, w_glu=w_glu, b_glu=b_glu, g_attn_out=g_attn_out,
                   g_ssm_out=g_ssm_out, w_out=w_out, g_post_mix=g_post_mix,
                   g_pre_ffn=g_pre_ffn, w_up=w_up, w_down=w_down, g_post_ffn=g_post_ffn)
    depth = w_in.shape[0]
    assert x_sample.shape[1] == 1, "decode path handles one new token per sequence"
    xp, xs = x_prompt, x_sample
    outs = []
    for l in range(depth):
        p = {n: a[l] for n, a in weights.items()}
        res = _layer(xp, xs, c_prompt, c_sample, cache_k[l], cache_v[l],
                     state_ssm_re[l], state_ssm_im[l], page_table, p)
        xp, xs = res[0], res[1]
        outs.append(res[2:])
    stacked = tuple(jnp.stack([o[i] for o in outs]) for i in range(8))
    return (xp, xs) + stacked
```
